```python
import math
import jax, jax.numpy as jnp
from jax import lax
import numpy as np

D_MODEL = 1024
BATCH = 16
SEQ = 2048
DEPTH = 1

BLOCK = 128
EPS = 1e-6
HEAD_DIM = 64
SWA_Q_HEADS = 8
SWA_KV_HEADS = 2
SWA_WINDOW = 128
N_BUCKETS = 32
MAX_DISTANCE = 128
SB_HEADS = 8
MEM_LEN = 256
MEM_HEADS = 4
MEM_HEAD_DIM = 128
SWA_Q_W = SWA_Q_HEADS * HEAD_DIM
SWA_KV_W = SWA_KV_HEADS * HEAD_DIM
SB_W = SB_HEADS * HEAD_DIM
MEM_W = MEM_HEADS * MEM_HEAD_DIM
N_BRANCH = 3
IN_SPLITS = (SWA_Q_W, SWA_KV_W, SWA_KV_W, SB_W, SB_W, SB_W, MEM_W, N_BRANCH * D_MODEL)
IN_W = sum(IN_SPLITS)
D_FF = -(-8 * D_MODEL // (3 * 256)) * 256

kernel_name = "hybrid_gated_swa_stickbreak_memxattn_swiglu"


def rms_norm(x, g):
    xf = x.astype(jnp.float32)
    y = xf * lax.rsqrt(jnp.mean(xf * xf, axis=-1, keepdims=True) + EPS)
    return (y * g.astype(jnp.float32)).astype(x.dtype)


def t5_bucket(dist):
    max_exact = N_BUCKETS // 2
    d = jnp.maximum(dist, 0)
    df = jnp.maximum(d, 1).astype(jnp.float32)
    large = max_exact + (jnp.log(df / max_exact) / math.log(MAX_DISTANCE / max_exact)
                         * (N_BUCKETS - max_exact)).astype(jnp.int32)
    large = jnp.minimum(large, N_BUCKETS - 1)
    return jnp.where(d < max_exact, d, large)


def swa_sink_attention(q, k, v, sinks, rel_bias):
    B, S, Hq, d = q.shape
    Hkv = k.shape[2]
    G = Hq // Hkv
    nb = S // BLOCK
    qb = q.reshape(B, nb, BLOCK, Hkv, G, d)
    kb = k.reshape(B, nb, BLOCK, Hkv, d)
    vb = v.reshape(B, nb, BLOCK, Hkv, d)
    kband = jnp.concatenate([jnp.concatenate([jnp.zeros_like(kb[:, :1]), kb[:, :-1]], axis=1), kb], axis=2)
    vband = jnp.concatenate([jnp.concatenate([jnp.zeros_like(vb[:, :1]), vb[:, :-1]], axis=1), vb], axis=2)
    scores = jnp.einsum('bnqhgd,bnkhd->bnhgqk', qb, kband).astype(jnp.float32) * (d ** -0.5)
    dist = (jnp.arange(BLOCK)[:, None] + BLOCK) - jnp.arange(2 * BLOCK)[None, :]
    in_win = (dist >= 0) & (dist < SWA_WINDOW)
    bias = rel_bias.astype(jnp.float32)[t5_bucket(dist)]
    bias = bias.transpose(2, 0, 1).reshape(Hkv, G, BLOCK, 2 * BLOCK)
    k_abs = (jnp.arange(nb)[:, None] - 1) * BLOCK + jnp.arange(2 * BLOCK)[None, :]
    mask = in_win[None] & (k_abs >= 0)[:, None, :]
    scores = jnp.where(mask[None, :, None, None], scores + bias[None, None], -jnp.inf)
    sink = sinks.astype(jnp.float32).reshape(Hkv, G)[:, :, None, None]
    m = jnp.maximum(jnp.max(scores, axis=-1, keepdims=True), sink)
    p = jnp.exp(scores - m)
    w = p / (jnp.sum(p, axis=-1, keepdims=True) + jnp.exp(sink - m))
    out = jnp.einsum('bnhgqk,bnkhd->bnqhgd', w.astype(v.dtype), vband)
    return out.reshape(B, S, Hq * d)


def stick_breaking_attention(q, k, v):
    B, S, H, d = q.shape
    nb = S // BLOCK
    outs = []
    for i in range(nb):
        L = (i + 1) * BLOCK
        z = jnp.einsum('bqhd,bkhd->bhqk', q[:, i * BLOCK:L], k[:, :L]).astype(jnp.float32) * (d ** -0.5)
        t = i * BLOCK + jnp.arange(BLOCK)[:, None]
        causal = jnp.arange(L)[None, :] < t
        log_1m = jnp.where(causal, jax.nn.log_sigmoid(-z), 0.0)
        between = lax.cumsum(log_1m, axis=3, reverse=True) - log_1m
        a = jnp.where(causal, jnp.exp(jax.nn.log_sigmoid(z) + between), 0.0)
        outs.append(jnp.einsum('bhqk,bkhd->bqhd', a.astype(v.dtype), v[:, :L]))
    return jnp.concatenate(outs, axis=1).reshape(B, S, H * d)


def memory_cross_attention(q, mk, mv):
    B, S, H, d = q.shape
    z = jnp.einsum('bshd,bmhd->bhsm', q, mk).astype(jnp.float32) * (d ** -0.5)
    w = jax.nn.softmax(z, axis=-1)
    return jnp.einsum('bhsm,bmhd->bshd', w.astype(mv.dtype), mv).reshape(B, S, H * d)


def setup_inputs(seed: int = 0) -> dict:
    key = jax.random.key(seed)
    ks = jax.random.split(key, 20)
    f = jnp.float32

    def w(k, shape, fan_in):
        return jax.random.normal(k, shape, f) * fan_in ** -0.5

    def gain(k, n):
        return 1.0 + 0.01 * jax.random.normal(k, (DEPTH, n), f)

    return {
        "x": jax.random.normal(ks[0], (BATCH, SEQ, D_MODEL), f),
        "mem": jax.random.normal(ks[1], (BATCH, MEM_LEN, D_MODEL), f),
        "ln_mix_pre": gain(ks[2], D_MODEL),
        "ln_mix_post": gain(ks[3], D_MODEL),
        "w_in": w(ks[4], (DEPTH, D_MODEL, IN_W), D_MODEL),
        "swa_sinks": 0.5 * jax.random.normal(ks[5], (DEPTH, SWA_Q_HEADS), f),
        "rel_bias": 0.5 * jax.random.normal(ks[6], (N_BUCKETS, SWA_Q_HEADS), f),
        "ln_mem": gain(ks[7], D_MODEL),
        "w_mem_kv": w(ks[8], (DEPTH, D_MODEL, 2 * MEM_W), D_MODEL),
        "w_branch_swa": w(ks[9], (DEPTH, SWA_Q_W, D_MODEL), SWA_Q_W),
        "w_branch_sb": w(ks[10], (DEPTH, SB_W, D_MODEL), SB_W),
        "w_branch_mem": w(ks[11], (DEPTH, MEM_W, D_MODEL), MEM_W),
        "w_out": w(ks[12], (DEPTH, D_MODEL, D_MODEL), D_MODEL),
        "ln_ffn_pre": gain(ks[13], D_MODEL),
        "ln_ffn_post": gain(ks[14], D_MODEL),
        "w_gate": w(ks[15], (DEPTH, D_MODEL, D_FF), D_MODEL),
        "w_up": w(ks[16], (DEPTH, D_MODEL, D_FF), D_MODEL),
        "w_down": w(ks[17], (DEPTH, D_FF, D_MODEL), D_FF),
    }


def reference(x, mem, ln_mix_pre, ln_mix_post, w_in, swa_sinks, rel_bias, ln_mem, w_mem_kv,
              w_branch_swa, w_branch_sb, w_branch_mem, w_out, ln_ffn_pre, ln_ffn_post,
              w_gate, w_up, w_down):
    B, S, D = x.shape
    M = mem.shape[1]
    split_idx = list(np.cumsum(IN_SPLITS)[:-1])
    h = x
    for l in range(DEPTH):
        u = rms_norm(h, ln_mix_pre[l])
        proj = jnp.einsum('bsd,de->bse', u, w_in[l])
        qa, ka, va, qb, kb, vb, qm, gl = jnp.split(proj, split_idx, axis=-1)
        y_swa = swa_sink_attention(qa.reshape(B, S, SWA_Q_HEADS, HEAD_DIM),
                                   ka.reshape(B, S, SWA_KV_HEADS, HEAD_DIM),
                                   va.reshape(B, S, SWA_KV_HEADS, HEAD_DIM),
                                   swa_sinks[l], rel_bias)
        y_sb = stick_breaking_attention(qb.reshape(B, S, SB_HEADS, HEAD_DIM),
                                        kb.reshape(B, S, SB_HEADS, HEAD_DIM),
                                        vb.reshape(B, S, SB_HEADS, HEAD_DIM))
        mkv = jnp.einsum('bmd,de->bme', rms_norm(mem, ln_mem[l]), w_mem_kv[l])
        mk, mv = jnp.split(mkv, 2, axis=-1)
        y_mem = memory_cross_attention(qm.reshape(B, S, MEM_HEADS, MEM_HEAD_DIM),
                                       mk.reshape(B, M, MEM_HEADS, MEM_HEAD_DIM),
                                       mv.reshape(B, M, MEM_HEADS, MEM_HEAD_DIM))
        g = jax.nn.sigmoid(gl.reshape(B, S, N_BRANCH, D))
        merged = (g[:, :, 0] * jnp.einsum('bse,ed->bsd', y_swa, w_branch_swa[l])
                  + g[:, :, 1] * jnp.einsum('bse,ed->bsd', y_sb, w_branch_sb[l])
                  + g[:, :, 2] * jnp.einsum('bse,ed->bsd', y_mem, w_branch_mem[l]))
        mix = jnp.einsum('bsd,de->bse', merged, w_out[l])
        h = h + rms_norm(mix, ln_mix_post[l])
        u = rms_norm(h, ln_ffn_pre[l])
        a = jax.nn.silu(jnp.einsum('bsd,df->bsf', u, w_gate[l])) * jnp.einsum('bsd,df->bsf', u, w_up[l])
        ffn = jnp.einsum('bsf,fd->bsd', a, w_down[l])
        h = h + rms_norm(ffn, ln_ffn_post[l])
    return h
```

```python
import functools
import math

import numpy as np
import jax
import jax.numpy as jnp
from jax import lax
from jax.experimental import pallas as pl
from jax.experimental.pallas import tpu as pltpu

F32 = jnp.float32
BF16 = jnp.bfloat16

D_MODEL = 1024
BLOCK = 128
EPS = 1e-6
HEAD_DIM = 64
SWA_Q_HEADS = 8
SWA_KV_HEADS = 2
SWA_WINDOW = 128
N_BUCKETS = 32
MAX_DISTANCE = 128
SB_HEADS = 8
MEM_HEADS = 4
MEM_HEAD_DIM = 128
SWA_Q_W = SWA_Q_HEADS * HEAD_DIM
SWA_KV_W = SWA_KV_HEADS * HEAD_DIM
SB_W = SB_HEADS * HEAD_DIM
MEM_W = MEM_HEADS * MEM_HEAD_DIM
N_BRANCH = 3
D_FF = 2816

LANES = 128
VMEM_LIMIT_BYTES = 56 * 1024 * 1024

TM_PROJ = 512
TM_MERGE = 512
TM_FFN = 512
TQ_SB = 256
KB_SB = 256
TQ_MEM = 512
COL_CHUNK = 512
FF_CHUNKS = ((0, 1024), (1024, 1024), (2048, 768))

PROJ_PIECES = (("qa", SWA_Q_W), ("ka", SWA_KV_W), ("va", SWA_KV_W), ("qb", SB_W),
               ("kb", SB_W), ("vb", SB_W), ("qm", MEM_W), ("gl", N_BRANCH * D_MODEL))

SWA_HEAD_ORDER = (0, 4, 1, 5, 2, 6, 3, 7)


def _rms_normalize(x, gain):
    ms = jnp.mean(x * x, axis=-1, keepdims=True)
    return x * lax.rsqrt(ms + EPS) * gain


def _half_mask(shape, half):
    lane = lax.broadcasted_iota(jnp.int32, shape, len(shape) - 1)
    return (lane < HEAD_DIM) if half == 0 else (lane >= HEAD_DIM)


def _dot_nt(a, b):
    return lax.dot_general(a, b, (((1,), (1,)), ((), ())), preferred_element_type=F32)


def _dot(a, b):
    return jnp.dot(a, b, preferred_element_type=F32)


def _in_proj_kernel(x_ref, g_ref, w_ref, *out_refs):
    u = _rms_normalize(x_ref[...], g_ref[...]).astype(BF16)
    off = 0
    for out_ref, (_, width) in zip(out_refs, PROJ_PIECES):
        for c in range(0, width, COL_CHUNK):
            cw = min(COL_CHUNK, width - c)
            y = _dot(u, w_ref[:, off + c:off + c + cw])
            out_ref[:, c:c + cw] = y.astype(BF16)
        off += width


def _in_proj(x2d, gain, w_bf16):
    t = x2d.shape[0]
    in_w = w_bf16.shape[1]
    out_shape = [jax.ShapeDtypeStruct((t, w), BF16) for _, w in PROJ_PIECES]
    out_specs = [pl.BlockSpec((TM_PROJ, w), lambda i: (i, 0)) for _, w in PROJ_PIECES]
    return pl.pallas_call(
        _in_proj_kernel,
        grid=(t // TM_PROJ,),
        in_specs=[
            pl.BlockSpec((TM_PROJ, D_MODEL), lambda i: (i, 0)),
            pl.BlockSpec((1, D_MODEL), lambda i: (0, 0)),
            pl.BlockSpec((D_MODEL, in_w), lambda i: (0, 0), pipeline_mode=pl.Buffered(1)),
        ],
        out_specs=out_specs,
        out_shape=out_shape,
        compiler_params=pltpu.CompilerParams(
            dimension_semantics=("arbitrary",), vmem_limit_bytes=VMEM_LIMIT_BYTES),
        name="in_proj",
    )(x2d, gain, w_bf16)


def _swa_kernel(sink_ref, q_ref, kp_ref, kc_ref, vp_ref, vc_ref, bias_ref, o_ref):
    i = pl.program_id(1)
    kband = jnp.concatenate([kp_ref[...], kc_ref[...]], axis=0)
    vband = jnp.concatenate([vp_ref[...], vc_ref[...]], axis=0)
    col = lax.broadcasted_iota(jnp.int32, (BLOCK, 2 * BLOCK), 1)
    band_valid = jnp.logical_or(col >= BLOCK, i > 0)
    for p in range(SWA_Q_HEADS // 2):
        q2 = q_ref[:, p * LANES:(p + 1) * LANES]
        halves = []
        for e in range(2):
            head = SWA_HEAD_ORDER[2 * p + e]
            qe = jnp.where(_half_mask(q2.shape, e), q2, jnp.zeros_like(q2))
            s = _dot_nt(qe, kband) + bias_ref[head]
            s = jnp.where(band_valid, s, -jnp.inf)
            sink = sink_ref[head]
            m = jnp.maximum(jnp.max(s, axis=-1, keepdims=True), sink)
            pr = jnp.exp(s - m)
            denom = jnp.sum(pr, axis=-1, keepdims=True) + jnp.exp(sink - m)
            w = pr / denom
            halves.append(_dot(w.astype(BF16), vband))
        out = jnp.where(_half_mask(halves[0].shape, 0), halves[0], halves[1])
        o_ref[:, p * LANES:(p + 1) * LANES] = out.astype(BF16)


def _swa_attention(qa, ka, va, sinks, bias, batch, seq):
    nb = seq // BLOCK
    qa = qa.reshape(batch, seq, SWA_Q_W)
    ka = ka.reshape(batch, seq, SWA_KV_W)
    va = va.reshape(batch, seq, SWA_KV_W)
    cur = lambda b, i: (b, i, 0)
    prev = lambda b, i: (b, jnp.maximum(i - 1, 0), 0)
    return pl.pallas_call(
        _swa_kernel,
        grid=(batch, nb),
        in_specs=[
            pl.BlockSpec(memory_space=pltpu.SMEM),
            pl.BlockSpec((None, BLOCK, SWA_Q_W), cur),
            pl.BlockSpec((None, BLOCK, SWA_KV_W), prev),
            pl.BlockSpec((None, BLOCK, SWA_KV_W), cur),
            pl.BlockSpec((None, BLOCK, SWA_KV_W), prev),
            pl.BlockSpec((None, BLOCK, SWA_KV_W), cur),
            pl.BlockSpec((SWA_Q_HEADS, BLOCK, 2 * BLOCK), lambda b, i: (0, 0, 0)),
        ],
        out_specs=pl.BlockSpec((None, BLOCK, SWA_Q_W), cur),
        out_shape=jax.ShapeDtypeStruct((batch, seq, SWA_Q_W), BF16),
        compiler_params=pltpu.CompilerParams(
            dimension_semantics=("arbitrary", "arbitrary"), vmem_limit_bytes=VMEM_LIMIT_BYTES),
        name="swa_attention",
    )(sinks, qa, ka, ka, va, va, bias)


def _t5_bucket(dist):
    max_exact = N_BUCKETS // 2
    d = jnp.maximum(dist, 0)
    df = jnp.maximum(d, 1).astype(F32)
    large = max_exact + (jnp.log(df / max_exact) / math.log(MAX_DISTANCE / max_exact)
                         * (N_BUCKETS - max_exact)).astype(jnp.int32)
    large = jnp.minimum(large, N_BUCKETS - 1)
    return jnp.where(d < max_exact, d, large)


def _swa_bias_table(rel_bias):
    dist = (jnp.arange(BLOCK)[:, None] + BLOCK) - jnp.arange(2 * BLOCK)[None, :]
    in_win = (dist >= 0) & (dist < SWA_WINDOW)
    bias = rel_bias.astype(F32)[_t5_bucket(dist)]
    bias = jnp.where(in_win[:, :, None], bias, -jnp.inf)
    return bias.transpose(2, 0, 1)


def _sb_kernel(q_ref, k_ref, v_ref, o_ref, acc_ref, carry_ref):
    i = pl.program_id(2)
    q2 = q_ref[...]
    q_heads = [jnp.where(_half_mask(q2.shape, e), q2, jnp.zeros_like(q2)) for e in range(2)]
    r = lax.broadcasted_iota(jnp.int32, (KB_SB, KB_SB), 0)
    c = lax.broadcasted_iota(jnp.int32, (KB_SB, KB_SB), 1)
    suffix = jnp.where(r > c, 1.0, 0.0).astype(BF16)
    causal = c < r

    acc_ref[...] = jnp.zeros_like(acc_ref)
    carry_ref[...] = jnp.zeros_like(carry_ref)

    def visit(j, diagonal):
        start = pl.multiple_of(j * KB_SB, KB_SB)
        kblk = k_ref[pl.ds(start, KB_SB), :]
        vblk = v_ref[pl.ds(start, KB_SB), :]
        for e in range(2):
            z = _dot_nt(q_heads[e], kblk)
            sp = jnp.maximum(z, 0.0) + jnp.log(1.0 + jnp.exp(-jnp.abs(z)))
            if diagonal:
                sp = jnp.where(causal, sp, 0.0)
            within = _dot(sp.astype(BF16), suffix)
            carry = carry_ref[e]
            a = jnp.exp((z - sp) - (within + carry))
            if diagonal:
                a = jnp.where(causal, a, 0.0)
            acc_ref[e] += _dot(a.astype(BF16), vblk)
            carry_ref[e] = carry + within[:, 0:1] + sp[:, 0:1]

    visit(i, True)

    def body(n, _):
        visit(i - 1 - n, False)
        return 0

    lax.fori_loop(0, i, body, 0)
    out = jnp.where(_half_mask((TQ_SB, LANES), 0), acc_ref[0], acc_ref[1])
    o_ref[...] = out.astype(BF16)


def _sb_attention(qb, kb, vb, batch, seq):
    qb = qb.reshape(batch, seq, SB_W)
    kb = kb.reshape(batch, seq, SB_W)
    vb = vb.reshape(batch, seq, SB_W)
    assert TQ_SB == KB_SB
    q_map = lambda b, p, i: (b, i, p)
    kv_map = lambda b, p, i: (b, 0, p)
    return pl.pallas_call(
        _sb_kernel,
        grid=(batch, SB_HEADS // 2, seq // TQ_SB),
        in_specs=[
            pl.BlockSpec((None, TQ_SB, LANES), q_map),
            pl.BlockSpec((None, seq, LANES), kv_map),
            pl.BlockSpec((None, seq, LANES), kv_map),
        ],
        out_specs=pl.BlockSpec((None, TQ_SB, LANES), q_map),
        out_shape=jax.ShapeDtypeStruct((batch, seq, SB_W), BF16),
        scratch_shapes=[
            pltpu.VMEM((2, TQ_SB, LANES), F32),
            pltpu.VMEM((2, TQ_SB, 1), F32),
        ],
        compiler_params=pltpu.CompilerParams(
            dimension_semantics=("arbitrary", "arbitrary", "arbitrary"),
            vmem_limit_bytes=VMEM_LIMIT_BYTES),
        name="sb_attention",
    )(qb, kb, vb)


def _mem_kv_kernel(m_ref, g_ref, w_ref, k_ref, v_ref):
    u = _rms_normalize(m_ref[...], g_ref[...]).astype(BF16)
    k_ref[...] = _dot(u, w_ref[:, :MEM_W]).astype(BF16)
    v_ref[...] = _dot(u, w_ref[:, MEM_W:]).astype(BF16)


def _mem_kv(mem, gain, w_bf16):
    batch, mem_len, _ = mem.shape
    blk = lambda b: (b, 0, 0)
    return pl.pallas_call(
        _mem_kv_kernel,
        grid=(batch,),
        in_specs=[
            pl.BlockSpec((None, mem_len, D_MODEL), blk),
            pl.BlockSpec((1, D_MODEL), lambda b: (0, 0)),
            pl.BlockSpec((D_MODEL, 2 * MEM_W), lambda b: (0, 0)),
        ],
        out_specs=[pl.BlockSpec((None, mem_len, MEM_W), blk)] * 2,
        out_shape=[jax.ShapeDtypeStruct((batch, mem_len, MEM_W), BF16)] * 2,
        compiler_params=pltpu.CompilerParams(
            dimension_semantics=("arbitrary",), vmem_limit_bytes=VMEM_LIMIT_BYTES),
        name="mem_kv",
    )(mem, gain, w_bf16)


def _mem_attn_kernel(q_ref, k_ref, v_ref, o_ref):
    scale = MEM_HEAD_DIM ** -0.5
    for h in range(MEM_HEADS):
        sl = slice(h * MEM_HEAD_DIM, (h + 1) * MEM_HEAD_DIM)
        z = _dot_nt(q_ref[:, sl], k_ref[:, sl]) * scale
        m = jnp.max(z, axis=-1, keepdims=True)
        pr = jnp.exp(z - m)
        w = pr / jnp.sum(pr, axis=-1, keepdims=True)
        o_ref[:, sl] = _dot(w.astype(BF16), v_ref[:, sl]).astype(BF16)


def _mem_attention(qm, mk, mv, batch, seq):
    qm = qm.reshape(batch, seq, MEM_W)
    mem_len = mk.shape[1]
    q_map = lambda b, i: (b, i, 0)
    kv_map = lambda b, i: (b, 0, 0)
    return pl.pallas_call(
        _mem_attn_kernel,
        grid=(batch, seq // TQ_MEM),
        in_specs=[
            pl.BlockSpec((None, TQ_MEM, MEM_W), q_map),
            pl.BlockSpec((None, mem_len, MEM_W), kv_map),
            pl.BlockSpec((None, mem_len, MEM_W), kv_map),
        ],
        out_specs=pl.BlockSpec((None, TQ_MEM, MEM_W), q_map),
        out_shape=jax.ShapeDtypeStruct((batch, seq, MEM_W), BF16),
        compiler_params=pltpu.CompilerParams(
            dimension_semantics=("arbitrary", "arbitrary"), vmem_limit_bytes=VMEM_LIMIT_BYTES),
        name="mem_attention",
    )(qm, mk, mv)


def _merge_kernel(x_ref, ya_ref, yb_ref, ym_ref, gl_ref, wa_ref, wb_ref, wm_ref, wo_ref,
                  g_ref, o_ref):
    merged = None
    for n, (y_ref, w_ref) in enumerate(((ya_ref, wa_ref), (yb_ref, wb_ref), (ym_ref, wm_ref))):
        gate = jax.nn.sigmoid(gl_ref[:, n * D_MODEL:(n + 1) * D_MODEL].astype(F32))
        term = gate * _dot(y_ref[...], w_ref[...])
        merged = term if merged is None else merged + term
    mix = _dot(merged.astype(BF16), wo_ref[...])
    o_ref[...] = x_ref[...] + _rms_normalize(mix, g_ref[...])


def _merge(x2d, y_swa, y_sb, y_mem, gl, w_swa, w_sb, w_mem, w_out, gain):
    t = x2d.shape[0]
    row = lambda i: (i, 0)
    const = lambda i: (0, 0)
    return pl.pallas_call(
        _merge_kernel,
        grid=(t // TM_MERGE,),
        in_specs=[
            pl.BlockSpec((TM_MERGE, D_MODEL), row),
            pl.BlockSpec((TM_MERGE, SWA_Q_W), row),
            pl.BlockSpec((TM_MERGE, SB_W), row),
            pl.BlockSpec((TM_MERGE, MEM_W), row),
            pl.BlockSpec((TM_MERGE, N_BRANCH * D_MODEL), row),
            pl.BlockSpec((SWA_Q_W, D_MODEL), const),
            pl.BlockSpec((SB_W, D_MODEL), const),
            pl.BlockSpec((MEM_W, D_MODEL), const),
            pl.BlockSpec((D_MODEL, D_MODEL), const),
            pl.BlockSpec((1, D_MODEL), const),
        ],
        out_specs=pl.BlockSpec((TM_MERGE, D_MODEL), row),
        out_shape=jax.ShapeDtypeStruct((t, D_MODEL), F32),
        compiler_params=pltpu.CompilerParams(
            dimension_semantics=("arbitrary",), vmem_limit_bytes=VMEM_LIMIT_BYTES),
        name="merge",
    )(x2d, y_swa, y_sb, y_mem, gl, w_swa, w_sb, w_mem, w_out, gain)


def _ffn_kernel(h_ref, gpre_ref, wg_ref, wu_ref, wd_ref, gpost_ref, o_ref):
    h = h_ref[...]
    u = _rms_normalize(h, gpre_ref[...]).astype(BF16)
    acc = None
    for start, width in FF_CHUNKS:
        gate = _dot(u, wg_ref[:, start:start + width])
        up = _dot(u, wu_ref[:, start:start + width])
        act = (gate * jax.nn.sigmoid(gate) * up).astype(BF16)
        part = _dot(act, wd_ref[start:start + width, :])
        acc = part if acc is None else acc + part
    o_ref[...] = h + _rms_normalize(acc, gpost_ref[...])


def _ffn(h2d, g_pre, w_gate, w_up, w_down, g_post):
    t = h2d.shape[0]
    row = lambda i: (i, 0)
    const = lambda i: (0, 0)
    resident = functools.partial(pl.BlockSpec, index_map=const, pipeline_mode=pl.Buffered(1))
    return pl.pallas_call(
        _ffn_kernel,
        grid=(t // TM_FFN,),
        in_specs=[
            pl.BlockSpec((TM_FFN, D_MODEL), row),
            pl.BlockSpec((1, D_MODEL), const),
            resident((D_MODEL, D_FF)),
            resident((D_MODEL, D_FF)),
            resident((D_FF, D_MODEL)),
            pl.BlockSpec((1, D_MODEL), const),
        ],
        out_specs=pl.BlockSpec((TM_FFN, D_MODEL), row),
        out_shape=jax.ShapeDtypeStruct((t, D_MODEL), F32),
        compiler_params=pltpu.CompilerParams(
            dimension_semantics=("arbitrary",), vmem_limit_bytes=VMEM_LIMIT_BYTES),
        name="ffn",
    )(h2d, g_pre, w_gate, w_up, w_down, g_post)


def _prepare_w_in(w_in):
    scale = HEAD_DIM ** -0.5
    assert scale == 0.125
    offs = np.cumsum([0] + [w for _, w in PROJ_PIECES])
    qa = w_in[:, offs[0]:offs[1]].reshape(D_MODEL, SWA_Q_HEADS, HEAD_DIM)
    qa = (qa[:, np.array(SWA_HEAD_ORDER), :] * scale).reshape(D_MODEL, SWA_Q_W)
    qb = w_in[:, offs[3]:offs[4]] * scale
    w = jnp.concatenate([qa, w_in[:, offs[1]:offs[3]], qb, w_in[:, offs[4]:]], axis=1)
    return w.astype(BF16)


def _prepare_w_branch_swa(w):
    w = w.reshape(SWA_Q_HEADS, HEAD_DIM, D_MODEL)[np.array(SWA_HEAD_ORDER)]
    return w.reshape(SWA_Q_W, D_MODEL).astype(BF16)


def kernel(x, mem, ln_mix_pre, ln_mix_post, w_in, swa_sinks, rel_bias, ln_mem, w_mem_kv,
           w_branch_swa, w_branch_sb, w_branch_mem, w_out, ln_ffn_pre, ln_ffn_post,
           w_gate, w_up, w_down):
    batch, seq, d = x.shape
    depth = w_in.shape[0]
    h = x.reshape(batch * seq, d)
    bias = _swa_bias_table(rel_bias)
    for l in range(depth):
        qa, ka, va, qb, kb, vb, qm, gl = _in_proj(h, ln_mix_pre[l][None], _prepare_w_in(w_in[l]))
        y_swa = _swa_attention(qa, ka, va, swa_sinks[l].astype(F32), bias, batch, seq)
        y_sb = _sb_attention(qb, kb, vb, batch, seq)
        mk, mv = _mem_kv(mem, ln_mem[l][None], w_mem_kv[l].astype(BF16))
        y_mem = _mem_attention(qm, mk, mv, batch, seq)
        h = _merge(h, y_swa.reshape(batch * seq, SWA_Q_W), y_sb.reshape(batch * seq, SB_W),
                   y_mem.reshape(batch * seq, MEM_W), gl,
                   _prepare_w_branch_swa(w_branch_swa[l]), w_branch_sb[l].astype(BF16),
                   w_branch_mem[l].astype(BF16), w_out[l].astype(BF16), ln_mix_post[l][None])
        h = _ffn(h, ln_ffn_pre[l][None], w_gate[l].astype(BF16), w_up[l].astype(BF16),
                 w_down[l].astype(BF16), ln_ffn_post[l][None])
    return h.reshape(batch, seq, d)
```

```python
import functools
import math

import numpy as np
import jax
import jax.numpy as jnp
from jax import lax
from jax.experimental import pallas as pl
from jax.experimental.pallas import tpu as pltpu

F32 = jnp.float32
BF16 = jnp.bfloat16

D_MODEL = 1024
BLOCK = 128
EPS = 1e-6
HEAD_DIM = 64
SWA_Q_HEADS = 8
SWA_KV_HEADS = 2
SWA_WINDOW = 128
N_BUCKETS = 32
MAX_DISTANCE = 128
SB_HEADS = 8
MEM_HEADS = 4
MEM_HEAD_DIM = 128
SWA_Q_W = SWA_Q_HEADS * HEAD_DIM
SWA_KV_W = SWA_KV_HEADS * HEAD_DIM
SB_W = SB_HEADS * HEAD_DIM
MEM_W = MEM_HEADS * MEM_HEAD_DIM
N_BRANCH = 3
D_FF = 2816

LANES = 128
VMEM_LIMIT_BYTES = 56 * 1024 * 1024

TM_PROJ = 512
TM_MERGE = 512
TM_FFN = 512
TQ_SB = 256
KB_SB = 256
TQ_MEM = 512
COL_CHUNK = 512
FF_CHUNKS = ((0, 1024), (1024, 1024), (2048, 768))

PROJ_PIECES = (("qa", SWA_Q_W), ("ka", SWA_KV_W), ("va", SWA_KV_W), ("qb", SB_W),
               ("kb", SB_W), ("vb", SB_W), ("qm", MEM_W), ("gl", N_BRANCH * D_MODEL))

SWA_HEAD_ORDER = (0, 4, 1, 5, 2, 6, 3, 7)


def _rms_normalize(x, gain):
    ms = jnp.mean(x * x, axis=-1, keepdims=True)
    return x * lax.rsqrt(ms + EPS) * gain


def _half_mask(shape, half):
    lane = lax.broadcasted_iota(jnp.int32, shape, len(shape) - 1)
    return (lane < HEAD_DIM) if half == 0 else (lane >= HEAD_DIM)


def _dot_nt(a, b):
    return lax.dot_general(a, b, (((1,), (1,)), ((), ())), preferred_element_type=F32)


def _dot(a, b):
    return jnp.dot(a, b, preferred_element_type=F32)


def _in_proj_kernel(x_ref, g_ref, w_ref, *out_refs):
    u = _rms_normalize(x_ref[...], g_ref[...]).astype(BF16)
    off = 0
    for out_ref, (_, width) in zip(out_refs, PROJ_PIECES):
        for c in range(0, width, COL_CHUNK):
            cw = min(COL_CHUNK, width - c)
            y = _dot(u, w_ref[:, off + c:off + c + cw])
            out_ref[:, c:c + cw] = y.astype(BF16)
        off += width


def _in_proj(x2d, gain, w_bf16):
    t = x2d.shape[0]
    in_w = w_bf16.shape[1]
    out_shape = [jax.ShapeDtypeStruct((t, w), BF16) for _, w in PROJ_PIECES]
    out_specs = [pl.BlockSpec((TM_PROJ, w), lambda i: (i, 0)) for _, w in PROJ_PIECES]
    return pl.pallas_call(
        _in_proj_kernel,
        grid=(t // TM_PROJ,),
        in_specs=[
            pl.BlockSpec((TM_PROJ, D_MODEL), lambda i: (i, 0)),
            pl.BlockSpec((1, D_MODEL), lambda i: (0, 0)),
            pl.BlockSpec((D_MODEL, in_w), lambda i: (0, 0), pipeline_mode=pl.Buffered(1)),
        ],
        out_specs=out_specs,
        out_shape=out_shape,
        compiler_params=pltpu.CompilerParams(
            dimension_semantics=("arbitrary",), vmem_limit_bytes=VMEM_LIMIT_BYTES),
        name="in_proj",
    )(x2d, gain, w_bf16)


def _swa_kernel(sink_ref, q_ref, kp_ref, kc_ref, vp_ref, vc_ref, bias_ref, o_ref):
    i = pl.program_id(1)
    kband = jnp.concatenate([kp_ref[...], kc_ref[...]], axis=0)
    vband = jnp.concatenate([vp_ref[...], vc_ref[...]], axis=0)
    col = lax.broadcasted_iota(jnp.int32, (BLOCK, 2 * BLOCK), 1)
    band_valid = jnp.logical_or(col >= BLOCK, i > 0)
    for p in range(SWA_Q_HEADS // 2):
        q2 = q_ref[:, p * LANES:(p + 1) * LANES]
        halves = []
        for e in range(2):
            head = SWA_HEAD_ORDER[2 * p + e]
            qe = jnp.where(_half_mask(q2.shape, e), q2, jnp.zeros_like(q2))
            s = _dot_nt(qe, kband) + bias_ref[head]
            s = jnp.where(band_valid, s, -jnp.inf)
            sink = sink_ref[head]
            m = jnp.maximum(jnp.max(s, axis=-1, keepdims=True), sink)
            pr = jnp.exp(s - m)
            denom = jnp.sum(pr, axis=-1, keepdims=True) + jnp.exp(sink - m)
            w = pr / denom
            halves.append(_dot(w.astype(BF16), vband))
        out = jnp.where(_half_mask(halves[0].shape, 0), halves[0], halves[1])
        o_ref[:, p * LANES:(p + 1) * LANES] = out.astype(BF16)


def _swa_attention(qa, ka, va, sinks, bias, batch, seq):
    nb = seq // BLOCK
    qa = qa.reshape(batch, seq, SWA_Q_W)
    ka = ka.reshape(batch, seq, SWA_KV_W)
    va = va.reshape(batch, seq, SWA_KV_W)
    cur = lambda b, i: (b, i, 0)
    prev = lambda b, i: (b, jnp.maximum(i - 1, 0), 0)
    return pl.pallas_call(
        _swa_kernel,
        grid=(batch, nb),
        in_specs=[
            pl.BlockSpec(memory_space=pltpu.SMEM),
            pl.BlockSpec((None, BLOCK, SWA_Q_W), cur),
            pl.BlockSpec((None, BLOCK, SWA_KV_W), prev),
            pl.BlockSpec((None, BLOCK, SWA_KV_W), cur),
            pl.BlockSpec((None, BLOCK, SWA_KV_W), prev),
            pl.BlockSpec((None, BLOCK, SWA_KV_W), cur),
            pl.BlockSpec((SWA_Q_HEADS, BLOCK, 2 * BLOCK), lambda b, i: (0, 0, 0)),
        ],
        out_specs=pl.BlockSpec((None, BLOCK, SWA_Q_W), cur),
        out_shape=jax.ShapeDtypeStruct((batch, seq, SWA_Q_W), BF16),
        compiler_params=pltpu.CompilerParams(
            dimension_semantics=("arbitrary", "arbitrary"), vmem_limit_bytes=VMEM_LIMIT_BYTES),
        name="swa_attention",
    )(sinks, qa, ka, ka, va, va, bias)


def _t5_bucket(dist):
    max_exact = N_BUCKETS // 2
    d = jnp.maximum(dist, 0)
    df = jnp.maximum(d, 1).astype(F32)
    large = max_exact + (jnp.log(df / max_exact) / math.log(MAX_DISTANCE / max_exact)
                         * (N_BUCKETS - max_exact)).astype(jnp.int32)
    large = jnp.minimum(large, N_BUCKETS - 1)
    return jnp.where(d < max_exact, d, large)


def _swa_bias_table(rel_bias):
    dist = (jnp.arange(BLOCK)[:, None] + BLOCK) - jnp.arange(2 * BLOCK)[None, :]
    in_win = (dist >= 0) & (dist < SWA_WINDOW)
    bucket = _t5_bucket(dist)
    rb = rel_bias.astype(F32)
    bias = jnp.zeros((SWA_Q_HEADS, BLOCK, 2 * BLOCK), F32)
    for b in range(N_BUCKETS):
        bias = jnp.where((bucket == b)[None], rb[b][:, None, None], bias)
    return jnp.where(in_win[None], bias, -jnp.inf)


def _neg_abs(x):
    bits = lax.bitcast_convert_type(x, jnp.int32) | jnp.int32(-2 ** 31)
    return lax.bitcast_convert_type(bits, F32)


def _sb_kernel(q_ref, k_ref, v_ref, o_ref, qs_ref, acc_ref, carry_ref):
    i = pl.program_id(1)
    n_pairs = SB_HEADS // 2
    half0 = _half_mask((TQ_SB, LANES), 0)
    r = lax.broadcasted_iota(jnp.int32, (KB_SB, KB_SB), 0)
    c = lax.broadcasted_iota(jnp.int32, (KB_SB, KB_SB), 1)
    suffix = jnp.where(r > c, 1.0, 0.0).astype(BF16)
    rr = lax.broadcasted_iota(jnp.int32, (2 * TQ_SB, KB_SB), 0)
    cc = lax.broadcasted_iota(jnp.int32, (2 * TQ_SB, KB_SB), 1)
    causal = cc < jnp.where(rr >= TQ_SB, rr - TQ_SB, rr)

    for p in range(n_pairs):
        q2 = q_ref[:, p * LANES:(p + 1) * LANES]
        zero = jnp.zeros_like(q2)
        qs_ref[p, :TQ_SB, :] = jnp.where(half0, q2, zero)
        qs_ref[p, TQ_SB:, :] = jnp.where(half0, zero, q2)
    acc_ref[...] = jnp.zeros_like(acc_ref)
    carry_ref[...] = jnp.zeros_like(carry_ref)

    def visit(j, diagonal):
        start = pl.multiple_of(j * KB_SB, KB_SB)
        lanes = [slice(p * LANES, (p + 1) * LANES) for p in range(n_pairs)]
        zs = [_dot_nt(qs_ref[p], k_ref[pl.ds(start, KB_SB), lanes[p]]) for p in range(n_pairs)]
        sps, lzs, firsts = [], [], []
        for p in range(n_pairs):
            z = zs[p]
            sp = jnp.maximum(z, 0.0) + jnp.log(1.0 + jnp.exp(_neg_abs(z)))
            lz = z - sp
            if diagonal:
                sp = jnp.where(causal, sp, 0.0)
                lz = jnp.where(causal, lz, -jnp.inf)
            sps.append(sp.astype(BF16))
            lzs.append(lz)
            firsts.append(sp[:, 0:1])
        within_all = _dot(jnp.concatenate(sps, axis=0), suffix)
        for p in range(n_pairs):
            within = within_all[p * 2 * TQ_SB:(p + 1) * 2 * TQ_SB]
            carry = carry_ref[p]
            a = jnp.exp(lzs[p] - (within + carry))
            pv = _dot(a.astype(BF16), v_ref[pl.ds(start, KB_SB), lanes[p]])
            acc_ref[p] += jnp.where(half0, pv[:TQ_SB], pv[TQ_SB:])
            carry_ref[p] = carry + within[:, 0:1] + firsts[p]

    visit(i, True)

    def body(n, _):
        visit(i - 1 - n, False)
        return 0

    lax.fori_loop(0, i, body, 0)
    for p in range(n_pairs):
        o_ref[:, p * LANES:(p + 1) * LANES] = acc_ref[p].astype(BF16)


def _sb_attention(qb, kb, vb, batch, seq):
    qb = qb.reshape(batch, seq, SB_W)
    kb = kb.reshape(batch, seq, SB_W)
    vb = vb.reshape(batch, seq, SB_W)
    assert TQ_SB == KB_SB
    q_map = lambda b, i: (b, i, 0)
    kv_map = lambda b, i: (b, 0, 0)
    return pl.pallas_call(
        _sb_kernel,
        grid=(batch, seq // TQ_SB),
        in_specs=[
            pl.BlockSpec((None, TQ_SB, SB_W), q_map),
            pl.BlockSpec((None, seq, SB_W), kv_map),
            pl.BlockSpec((None, seq, SB_W), kv_map),
        ],
        out_specs=pl.BlockSpec((None, TQ_SB, SB_W), q_map),
        out_shape=jax.ShapeDtypeStruct((batch, seq, SB_W), BF16),
        scratch_shapes=[
            pltpu.VMEM((SB_HEADS // 2, 2 * TQ_SB, LANES), BF16),
            pltpu.VMEM((SB_HEADS // 2, TQ_SB, LANES), F32),
            pltpu.VMEM((SB_HEADS // 2, 2 * TQ_SB, 1), F32),
        ],
        compiler_params=pltpu.CompilerParams(
            dimension_semantics=("arbitrary", "arbitrary"),
            vmem_limit_bytes=VMEM_LIMIT_BYTES),
        name="sb_attention",
    )(qb, kb, vb)


def _mem_kv_kernel(m_ref, g_ref, w_ref, k_ref, v_ref):
    u = _rms_normalize(m_ref[...], g_ref[...]).astype(BF16)
    k_ref[...] = _dot(u, w_ref[:, :MEM_W]).astype(BF16)
    v_ref[...] = _dot(u, w_ref[:, MEM_W:]).astype(BF16)


def _mem_kv(mem, gain, w_bf16):
    batch, mem_len, _ = mem.shape
    blk = lambda b: (b, 0, 0)
    return pl.pallas_call(
        _mem_kv_kernel,
        grid=(batch,),
        in_specs=[
            pl.BlockSpec((None, mem_len, D_MODEL), blk),
            pl.BlockSpec((1, D_MODEL), lambda b: (0, 0)),
            pl.BlockSpec((D_MODEL, 2 * MEM_W), lambda b: (0, 0)),
        ],
        out_specs=[pl.BlockSpec((None, mem_len, MEM_W), blk)] * 2,
        out_shape=[jax.ShapeDtypeStruct((batch, mem_len, MEM_W), BF16)] * 2,
        compiler_params=pltpu.CompilerParams(
            dimension_semantics=("arbitrary",), vmem_limit_bytes=VMEM_LIMIT_BYTES),
        name="mem_kv",
    )(mem, gain, w_bf16)


def _mem_attn_kernel(q_ref, k_ref, v_ref, o_ref):
    scale = MEM_HEAD_DIM ** -0.5
    for h in range(MEM_HEADS):
        sl = slice(h * MEM_HEAD_DIM, (h + 1) * MEM_HEAD_DIM)
        z = _dot_nt(q_ref[:, sl], k_ref[:, sl]) * scale
        m = jnp.max(z, axis=-1, keepdims=True)
        pr = jnp.exp(z - m)
        w = pr / jnp.sum(pr, axis=-1, keepdims=True)
        o_ref[:, sl] = _dot(w.astype(BF16), v_ref[:, sl]).astype(BF16)


def _mem_attention(qm, mk, mv, batch, seq):
    qm = qm.reshape(batch, seq, MEM_W)
    mem_len = mk.shape[1]
    q_map = lambda b, i: (b, i, 0)
    kv_map = lambda b, i: (b, 0, 0)
    return pl.pallas_call(
        _mem_attn_kernel,
        grid=(batch, seq // TQ_MEM),
        in_specs=[
            pl.BlockSpec((None, TQ_MEM, MEM_W), q_map),
            pl.BlockSpec((None, mem_len, MEM_W), kv_map),
            pl.BlockSpec((None, mem_len, MEM_W), kv_map),
        ],
        out_specs=pl.BlockSpec((None, TQ_MEM, MEM_W), q_map),
        out_shape=jax.ShapeDtypeStruct((batch, seq, MEM_W), BF16),
        compiler_params=pltpu.CompilerParams(
            dimension_semantics=("arbitrary", "arbitrary"), vmem_limit_bytes=VMEM_LIMIT_BYTES),
        name="mem_attention",
    )(qm, mk, mv)


def _merge_kernel(x_ref, ya_ref, yb_ref, ym_ref, gl_ref, wa_ref, wb_ref, wm_ref, wo_ref,
                  g_ref, o_ref):
    merged = None
    for n, (y_ref, w_ref) in enumerate(((ya_ref, wa_ref), (yb_ref, wb_ref), (ym_ref, wm_ref))):
        gate = jax.nn.sigmoid(gl_ref[:, n * D_MODEL:(n + 1) * D_MODEL].astype(F32))
        term = gate * _dot(y_ref[...], w_ref[...])
        merged = term if merged is None else merged + term
    mix = _dot(merged.astype(BF16), wo_ref[...])
    o_ref[...] = x_ref[...] + _rms_normalize(mix, g_ref[...])


def _merge(x2d, y_swa, y_sb, y_mem, gl, w_swa, w_sb, w_mem, w_out, gain):
    t = x2d.shape[0]
    row = lambda i: (i, 0)
    const = lambda i: (0, 0)
    return pl.pallas_call(
        _merge_kernel,
        grid=(t // TM_MERGE,),
        in_specs=[
            pl.BlockSpec((TM_MERGE, D_MODEL), row),
            pl.BlockSpec((TM_MERGE, SWA_Q_W), row),
            pl.BlockSpec((TM_MERGE, SB_W), row),
            pl.BlockSpec((TM_MERGE, MEM_W), row),
            pl.BlockSpec((TM_MERGE, N_BRANCH * D_MODEL), row),
            pl.BlockSpec((SWA_Q_W, D_MODEL), const),
            pl.BlockSpec((SB_W, D_MODEL), const),
            pl.BlockSpec((MEM_W, D_MODEL), const),
            pl.BlockSpec((D_MODEL, D_MODEL), const),
            pl.BlockSpec((1, D_MODEL), const),
        ],
        out_specs=pl.BlockSpec((TM_MERGE, D_MODEL), row),
        out_shape=jax.ShapeDtypeStruct((t, D_MODEL), F32),
        compiler_params=pltpu.CompilerParams(
            dimension_semantics=("arbitrary",), vmem_limit_bytes=VMEM_LIMIT_BYTES),
        name="merge",
    )(x2d, y_swa, y_sb, y_mem, gl, w_swa, w_sb, w_mem, w_out, gain)


def _ffn_kernel(h_ref, gpre_ref, wg_ref, wu_ref, wd_ref, gpost_ref, o_ref):
    h = h_ref[...]
    u = _rms_normalize(h, gpre_ref[...]).astype(BF16)
    acc = None
    for start, width in FF_CHUNKS:
        gate = _dot(u, wg_ref[:, start:start + width])
        up = _dot(u, wu_ref[:, start:start + width])
        act = (gate * jax.nn.sigmoid(gate) * up).astype(BF16)
        part = _dot(act, wd_ref[start:start + width, :])
        acc = part if acc is None else acc + part
    o_ref[...] = h + _rms_normalize(acc, gpost_ref[...])


def _ffn(h2d, g_pre, w_gate, w_up, w_down, g_post):
    t = h2d.shape[0]
    row = lambda i: (i, 0)
    const = lambda i: (0, 0)
    resident = functools.partial(pl.BlockSpec, index_map=const, pipeline_mode=pl.Buffered(1))
    return pl.pallas_call(
        _ffn_kernel,
        grid=(t // TM_FFN,),
        in_specs=[
            pl.BlockSpec((TM_FFN, D_MODEL), row),
            pl.BlockSpec((1, D_MODEL), const),
            resident((D_MODEL, D_FF)),
            resident((D_MODEL, D_FF)),
            resident((D_FF, D_MODEL)),
            pl.BlockSpec((1, D_MODEL), const),
        ],
        out_specs=pl.BlockSpec((TM_FFN, D_MODEL), row),
        out_shape=jax.ShapeDtypeStruct((t, D_MODEL), F32),
        compiler_params=pltpu.CompilerParams(
            dimension_semantics=("arbitrary",), vmem_limit_bytes=VMEM_LIMIT_BYTES),
        name="ffn",
    )(h2d, g_pre, w_gate, w_up, w_down, g_post)


def _prepare_w_in(w_in):
    scale = HEAD_DIM ** -0.5
    assert scale == 0.125
    offs = np.cumsum([0] + [w for _, w in PROJ_PIECES])
    qa = w_in[:, offs[0]:offs[1]].reshape(D_MODEL, SWA_Q_HEADS, HEAD_DIM)
    qa = (qa[:, np.array(SWA_HEAD_ORDER), :] * scale).reshape(D_MODEL, SWA_Q_W)
    qb = w_in[:, offs[3]:offs[4]] * scale
    w = jnp.concatenate([qa, w_in[:, offs[1]:offs[3]], qb, w_in[:, offs[4]:]], axis=1)
    return w.astype(BF16)


def _prepare_w_branch_swa(w):
    w = w.reshape(SWA_Q_HEADS, HEAD_DIM, D_MODEL)[np.array(SWA_HEAD_ORDER)]
    return w.reshape(SWA_Q_W, D_MODEL).astype(BF16)


def kernel(x, mem, ln_mix_pre, ln_mix_post, w_in, swa_sinks, rel_bias, ln_mem, w_mem_kv,
           w_branch_swa, w_branch_sb, w_branch_mem, w_out, ln_ffn_pre, ln_ffn_post,
           w_gate, w_up, w_down):
    batch, seq, d = x.shape
    depth = w_in.shape[0]
    h = x.reshape(batch * seq, d)
    bias = _swa_bias_table(rel_bias)
    for l in range(depth):
        qa, ka, va, qb, kb, vb, qm, gl = _in_proj(h, ln_mix_pre[l][None], _prepare_w_in(w_in[l]))
        y_swa = _swa_attention(qa, ka, va, swa_sinks[l].astype(F32), bias, batch, seq)
        y_sb = _sb_attention(qb, kb, vb, batch, seq)
        mk, mv = _mem_kv(mem, ln_mem[l][None], w_mem_kv[l].astype(BF16))
        y_mem = _mem_attention(qm, mk, mv, batch, seq)
        h = _merge(h, y_swa.reshape(batch * seq, SWA_Q_W), y_sb.reshape(batch * seq, SB_W),
                   y_mem.reshape(batch * seq, MEM_W), gl,
                   _prepare_w_branch_swa(w_branch_swa[l]), w_branch_sb[l].astype(BF16),
                   w_branch_mem[l].astype(BF16), w_out[l].astype(BF16), ln_mix_post[l][None])
        h = _ffn(h, ln_ffn_pre[l][None], w_gate[l].astype(BF16), w_up[l].astype(BF16),
                 w_down[l].astype(BF16), ln_ffn_post[l][None])
    return h.reshape(batch, seq, d)
```

```python
import functools
import math

import numpy as np
import jax
import jax.numpy as jnp
from jax import lax
from jax.experimental import pallas as pl
from jax.experimental.pallas import tpu as pltpu

F32 = jnp.float32
BF16 = jnp.bfloat16

D_MODEL = 1024
BLOCK = 128
EPS = 1e-6
HEAD_DIM = 64
SWA_Q_HEADS = 8
SWA_KV_HEADS = 2
SWA_WINDOW = 128
N_BUCKETS = 32
MAX_DISTANCE = 128
SB_HEADS = 8
MEM_HEADS = 4
MEM_HEAD_DIM = 128
SWA_Q_W = SWA_Q_HEADS * HEAD_DIM
SWA_KV_W = SWA_KV_HEADS * HEAD_DIM
SB_W = SB_HEADS * HEAD_DIM
MEM_W = MEM_HEADS * MEM_HEAD_DIM
N_BRANCH = 3
D_FF = 2816

LANES = 128
VMEM_LIMIT_BYTES = 56 * 1024 * 1024

TM_PROJ = 512
TM_MERGE = 512
TM_FFN = 512
TQ_SB = 256
KB_SB = 256
TQ_MEM = 512
SWA_BLOCKS_PER_STEP = 4
COL_CHUNK = 512
SB_DEAD_CARRY = 128.0
FF_CHUNKS = ((0, 1024), (1024, 1024), (2048, 768))

PROJ_PIECES = (("qa", SWA_Q_W), ("ka", SWA_KV_W), ("va", SWA_KV_W), ("qb", SB_W),
               ("kb", SB_W), ("vb", SB_W), ("qm", MEM_W), ("gl", N_BRANCH * D_MODEL))

SWA_HEAD_ORDER = (0, 4, 1, 5, 2, 6, 3, 7)


def _rms_normalize(x, gain):
    ms = jnp.mean(x * x, axis=-1, keepdims=True)
    return x * lax.rsqrt(ms + EPS) * gain


def _half_mask(shape, half):
    lane = lax.broadcasted_iota(jnp.int32, shape, len(shape) - 1)
    return (lane < HEAD_DIM) if half == 0 else (lane >= HEAD_DIM)


def _dot_nt(a, b):
    return lax.dot_general(a, b, (((1,), (1,)), ((), ())), preferred_element_type=F32)


def _dot(a, b):
    return jnp.dot(a, b, preferred_element_type=F32)


def _in_proj_kernel(x_ref, g_ref, w_ref, *out_refs):
    u = _rms_normalize(x_ref[...], g_ref[...]).astype(BF16)
    off = 0
    for out_ref, (_, width) in zip(out_refs, PROJ_PIECES):
        for c in range(0, width, COL_CHUNK):
            cw = min(COL_CHUNK, width - c)
            y = _dot(u, w_ref[:, off + c:off + c + cw])
            out_ref[:, c:c + cw] = y.astype(BF16)
        off += width


def _in_proj(x2d, gain, w_bf16):
    t = x2d.shape[0]
    in_w = w_bf16.shape[1]
    out_shape = [jax.ShapeDtypeStruct((t, w), BF16) for _, w in PROJ_PIECES]
    out_specs = [pl.BlockSpec((TM_PROJ, w), lambda i: (i, 0)) for _, w in PROJ_PIECES]
    return pl.pallas_call(
        _in_proj_kernel,
        grid=(t // TM_PROJ,),
        in_specs=[
            pl.BlockSpec((TM_PROJ, D_MODEL), lambda i: (i, 0)),
            pl.BlockSpec((1, D_MODEL), lambda i: (0, 0)),
            pl.BlockSpec((D_MODEL, in_w), lambda i: (0, 0), pipeline_mode=pl.Buffered(1)),
        ],
        out_specs=out_specs,
        out_shape=out_shape,
        compiler_params=pltpu.CompilerParams(
            dimension_semantics=("arbitrary",), vmem_limit_bytes=VMEM_LIMIT_BYTES),
        name="in_proj",
    )(x2d, gain, w_bf16)


def _swa_kernel(sink_ref, q_ref, kp_ref, kc_ref, vp_ref, vc_ref, bias_ref, o_ref, qs_ref):
    i = pl.program_id(1)
    half0 = _half_mask((BLOCK, LANES), 0)
    r = lax.broadcasted_iota(jnp.int32, (BLOCK, BLOCK), 0)
    c = lax.broadcasted_iota(jnp.int32, (BLOCK, BLOCK), 1)
    from_prev = c > r
    missing = jnp.logical_and(from_prev, i == 0)
    rows = SWA_Q_HEADS * BLOCK

    scores, vbands = [], []
    for j in range(SWA_BLOCKS_PER_STEP):
        blk = slice(j * BLOCK, (j + 1) * BLOCK)
        for p in range(SWA_Q_HEADS // 2):
            q2 = q_ref[blk, p * LANES:(p + 1) * LANES]
            zero = jnp.zeros_like(q2)
            base = j * rows + 2 * p * BLOCK
            qs_ref[base:base + BLOCK, :] = jnp.where(half0, q2, zero)
            qs_ref[base + BLOCK:base + 2 * BLOCK, :] = jnp.where(half0, zero, q2)
        if j == 0:
            kband = jnp.concatenate([kp_ref[...], kc_ref[blk, :]], axis=0)
            vband = jnp.concatenate([vp_ref[...], vc_ref[blk, :]], axis=0)
        else:
            band = slice((j - 1) * BLOCK, (j + 1) * BLOCK)
            kband, vband = kc_ref[band, :], vc_ref[band, :]
        scores.append(_dot_nt(qs_ref[j * rows:(j + 1) * rows, :], kband))
        vbands.append(vband)

    for j in range(SWA_BLOCKS_PER_STEP):
        z = scores[j]
        weights, recips = [], []
        for h in range(SWA_Q_HEADS):
            head = SWA_HEAD_ORDER[h]
            zh = z[h * BLOCK:(h + 1) * BLOCK]
            s = jnp.where(from_prev, zh[:, :BLOCK], zh[:, BLOCK:]) + bias_ref[head]
            if j == 0:
                s = jnp.where(missing, -jnp.inf, s)
            sink = sink_ref[head]
            m = jnp.maximum(jnp.max(s, axis=-1, keepdims=True), sink)
            pr = jnp.exp(s - m)
            denom = jnp.sum(pr, axis=-1, keepdims=True) + jnp.exp(sink - m)
            recips.append(1.0 / denom)
            zero = jnp.zeros_like(pr)
            unfolded = jnp.concatenate(
                [jnp.where(from_prev, pr, zero), jnp.where(from_prev, zero, pr)], axis=1)
            weights.append(unfolded.astype(BF16))
        o = _dot(jnp.concatenate(weights, axis=0), vbands[j])
        for p in range(SWA_Q_HEADS // 2):
            o0 = o[(2 * p) * BLOCK:(2 * p + 1) * BLOCK] * recips[2 * p]
            o1 = o[(2 * p + 1) * BLOCK:(2 * p + 2) * BLOCK] * recips[2 * p + 1]
            o_ref[j * BLOCK:(j + 1) * BLOCK, p * LANES:(p + 1) * LANES] = (
                jnp.where(half0, o0, o1).astype(BF16))


def _swa_attention(qa, ka, va, sinks, bias, batch, seq):
    tq = SWA_BLOCKS_PER_STEP * BLOCK
    qa = qa.reshape(batch, seq, SWA_Q_W)
    ka = ka.reshape(batch, seq, SWA_KV_W)
    va = va.reshape(batch, seq, SWA_KV_W)
    cur = lambda b, i: (b, i, 0)
    prev = lambda b, i: (b, jnp.maximum(i * SWA_BLOCKS_PER_STEP - 1, 0), 0)
    return pl.pallas_call(
        _swa_kernel,
        grid=(batch, seq // tq),
        in_specs=[
            pl.BlockSpec(memory_space=pltpu.SMEM),
            pl.BlockSpec((None, tq, SWA_Q_W), cur),
            pl.BlockSpec((None, BLOCK, SWA_KV_W), prev),
            pl.BlockSpec((None, tq, SWA_KV_W), cur),
            pl.BlockSpec((None, BLOCK, SWA_KV_W), prev),
            pl.BlockSpec((None, tq, SWA_KV_W), cur),
            pl.BlockSpec((SWA_Q_HEADS, BLOCK, BLOCK), lambda b, i: (0, 0, 0)),
        ],
        out_specs=pl.BlockSpec((None, tq, SWA_Q_W), cur),
        out_shape=jax.ShapeDtypeStruct((batch, seq, SWA_Q_W), BF16),
        scratch_shapes=[pltpu.VMEM((SWA_BLOCKS_PER_STEP * SWA_Q_HEADS * BLOCK, LANES), BF16)],
        compiler_params=pltpu.CompilerParams(
            dimension_semantics=("arbitrary", "arbitrary"), vmem_limit_bytes=VMEM_LIMIT_BYTES),
        name="swa_attention",
    )(sinks, qa, ka, ka, va, va, bias)


def _t5_bucket(dist):
    max_exact = N_BUCKETS // 2
    d = jnp.maximum(dist, 0)
    df = jnp.maximum(d, 1).astype(F32)
    large = max_exact + (jnp.log(df / max_exact) / math.log(MAX_DISTANCE / max_exact)
                         * (N_BUCKETS - max_exact)).astype(jnp.int32)
    large = jnp.minimum(large, N_BUCKETS - 1)
    return jnp.where(d < max_exact, d, large)


def _swa_bias_table(rel_bias):
    r = jnp.arange(BLOCK)[:, None]
    c = jnp.arange(BLOCK)[None, :]
    dist = jnp.where(c > r, r + BLOCK - c, r - c)
    assert SWA_WINDOW == BLOCK
    bucket = _t5_bucket(dist)
    rb = rel_bias.astype(F32)
    bias = jnp.zeros((SWA_Q_HEADS, BLOCK, BLOCK), F32)
    for b in range(N_BUCKETS):
        bias = jnp.where((bucket == b)[None], rb[b][:, None, None], bias)
    return bias


def _neg_abs(x):
    bits = lax.bitcast_convert_type(x, jnp.int32) | jnp.int32(-2 ** 31)
    return lax.bitcast_convert_type(bits, F32)


def _sb_kernel(q_ref, k_ref, v_ref, o_ref, qs_ref, acc_ref, carry_ref):
    i = pl.program_id(1)
    n_pairs = SB_HEADS // 2
    half0 = _half_mask((TQ_SB, LANES), 0)
    r = lax.broadcasted_iota(jnp.int32, (KB_SB, KB_SB), 0)
    c = lax.broadcasted_iota(jnp.int32, (KB_SB, KB_SB), 1)
    suffix = jnp.where(r > c, 1.0, 0.0).astype(BF16)
    rr = lax.broadcasted_iota(jnp.int32, (2 * TQ_SB, KB_SB), 0)
    cc = lax.broadcasted_iota(jnp.int32, (2 * TQ_SB, KB_SB), 1)
    causal = cc < jnp.where(rr >= TQ_SB, rr - TQ_SB, rr)

    for p in range(n_pairs):
        q2 = q_ref[:, p * LANES:(p + 1) * LANES]
        zero = jnp.zeros_like(q2)
        qs_ref[p, :TQ_SB, :] = jnp.where(half0, q2, zero)
        qs_ref[p, TQ_SB:, :] = jnp.where(half0, zero, q2)
    acc_ref[...] = jnp.zeros_like(acc_ref)
    carry_ref[...] = jnp.zeros_like(carry_ref)

    def visit(j, diagonal):
        start = pl.multiple_of(j * KB_SB, KB_SB)
        lanes = [slice(p * LANES, (p + 1) * LANES) for p in range(n_pairs)]
        zs = [_dot_nt(qs_ref[p], k_ref[pl.ds(start, KB_SB), lanes[p]]) for p in range(n_pairs)]
        sps, lzs, firsts = [], [], []
        for p in range(n_pairs):
            z = zs[p]
            sp = jnp.maximum(z, 0.0) + jnp.log(1.0 + jnp.exp(_neg_abs(z)))
            lz = z - sp
            if diagonal:
                sp = jnp.where(causal, sp, 0.0)
                lz = jnp.where(causal, lz, -jnp.inf)
            sps.append(sp.astype(BF16))
            lzs.append(lz)
            firsts.append(sp[:, 0:1])
        within_all = _dot(jnp.concatenate(sps, axis=0), suffix)
        for p in range(n_pairs):
            within = within_all[p * 2 * TQ_SB:(p + 1) * 2 * TQ_SB]
            carry = carry_ref[p]
            a = jnp.exp(lzs[p] - (within + carry))
            pv = _dot(a.astype(BF16), v_ref[pl.ds(start, KB_SB), lanes[p]])
            acc_ref[p] += jnp.where(half0, pv[:TQ_SB], pv[TQ_SB:])
            carry_ref[p] = carry + within[:, 0:1] + firsts[p]

    visit(i, True)

    def live(state):
        n, dead = state
        return jnp.logical_and(n < i, dead == 0)

    def body(state):
        n, _ = state
        visit(i - 1 - n, False)
        least = carry_ref[0]
        for p in range(1, n_pairs):
            least = jnp.minimum(least, carry_ref[p])
        dead = (jnp.min(least) >= SB_DEAD_CARRY).astype(jnp.int32)
        return n + 1, dead

    lax.while_loop(live, body, (jnp.int32(0), jnp.int32(0)))
    for p in range(n_pairs):
        o_ref[:, p * LANES:(p + 1) * LANES] = acc_ref[p].astype(BF16)


def _sb_attention(qb, kb, vb, batch, seq):
    qb = qb.reshape(batch, seq, SB_W)
    kb = kb.reshape(batch, seq, SB_W)
    vb = vb.reshape(batch, seq, SB_W)
    assert TQ_SB == KB_SB
    q_map = lambda b, i: (b, i, 0)
    kv_map = lambda b, i: (b, 0, 0)
    return pl.pallas_call(
        _sb_kernel,
        grid=(batch, seq // TQ_SB),
        in_specs=[
            pl.BlockSpec((None, TQ_SB, SB_W), q_map),
            pl.BlockSpec((None, seq, SB_W), kv_map),
            pl.BlockSpec((None, seq, SB_W), kv_map),
        ],
        out_specs=pl.BlockSpec((None, TQ_SB, SB_W), q_map),
        out_shape=jax.ShapeDtypeStruct((batch, seq, SB_W), BF16),
        scratch_shapes=[
            pltpu.VMEM((SB_HEADS // 2, 2 * TQ_SB, LANES), BF16),
            pltpu.VMEM((SB_HEADS // 2, TQ_SB, LANES), F32),
            pltpu.VMEM((SB_HEADS // 2, 2 * TQ_SB, 1), F32),
        ],
        compiler_params=pltpu.CompilerParams(
            dimension_semantics=("arbitrary", "arbitrary"),
            vmem_limit_bytes=VMEM_LIMIT_BYTES),
        name="sb_attention",
    )(qb, kb, vb)


def _mem_kv_kernel(m_ref, g_ref, w_ref, k_ref, v_ref):
    u = _rms_normalize(m_ref[...], g_ref[...]).astype(BF16)
    k_ref[...] = _dot(u, w_ref[:, :MEM_W]).astype(BF16)
    v_ref[...] = _dot(u, w_ref[:, MEM_W:]).astype(BF16)


def _mem_kv(mem, gain, w_bf16):
    batch, mem_len, _ = mem.shape
    blk = lambda b: (b, 0, 0)
    return pl.pallas_call(
        _mem_kv_kernel,
        grid=(batch,),
        in_specs=[
            pl.BlockSpec((None, mem_len, D_MODEL), blk),
            pl.BlockSpec((1, D_MODEL), lambda b: (0, 0)),
            pl.BlockSpec((D_MODEL, 2 * MEM_W), lambda b: (0, 0)),
        ],
        out_specs=[pl.BlockSpec((None, mem_len, MEM_W), blk)] * 2,
        out_shape=[jax.ShapeDtypeStruct((batch, mem_len, MEM_W), BF16)] * 2,
        compiler_params=pltpu.CompilerParams(
            dimension_semantics=("arbitrary",), vmem_limit_bytes=VMEM_LIMIT_BYTES),
        name="mem_kv",
    )(mem, gain, w_bf16)


def _mem_attn_kernel(q_ref, k_ref, v_ref, o_ref):
    c = (MEM_HEAD_DIM ** -0.5) * math.log2(math.e)
    heads = [slice(h * MEM_HEAD_DIM, (h + 1) * MEM_HEAD_DIM) for h in range(MEM_HEADS)]
    zs = [_dot_nt(q_ref[:, sl], k_ref[:, sl]) for sl in heads]
    for sl, z in zip(heads, zs):
        pr = jnp.exp2((z - jnp.max(z, axis=-1, keepdims=True)) * c)
        recip = 1.0 / jnp.sum(pr, axis=-1, keepdims=True)
        o_ref[:, sl] = (_dot(pr.astype(BF16), v_ref[:, sl]) * recip).astype(BF16)


def _mem_attention(qm, mk, mv, batch, seq):
    qm = qm.reshape(batch, seq, MEM_W)
    mem_len = mk.shape[1]
    q_map = lambda b, i: (b, i, 0)
    kv_map = lambda b, i: (b, 0, 0)
    return pl.pallas_call(
        _mem_attn_kernel,
        grid=(batch, seq // TQ_MEM),
        in_specs=[
            pl.BlockSpec((None, TQ_MEM, MEM_W), q_map),
            pl.BlockSpec((None, mem_len, MEM_W), kv_map),
            pl.BlockSpec((None, mem_len, MEM_W), kv_map),
        ],
        out_specs=pl.BlockSpec((None, TQ_MEM, MEM_W), q_map),
        out_shape=jax.ShapeDtypeStruct((batch, seq, MEM_W), BF16),
        compiler_params=pltpu.CompilerParams(
            dimension_semantics=("arbitrary", "arbitrary"), vmem_limit_bytes=VMEM_LIMIT_BYTES),
        name="mem_attention",
    )(qm, mk, mv)


def _merge_kernel(x_ref, ya_ref, yb_ref, ym_ref, gl_ref, wa_ref, wb_ref, wm_ref, wo_ref,
                  g_ref, o_ref):
    merged = None
    for n, (y_ref, w_ref) in enumerate(((ya_ref, wa_ref), (yb_ref, wb_ref), (ym_ref, wm_ref))):
        gate = jax.nn.sigmoid(gl_ref[:, n * D_MODEL:(n + 1) * D_MODEL].astype(F32))
        term = gate * _dot(y_ref[...], w_ref[...])
        merged = term if merged is None else merged + term
    mix = _dot(merged.astype(BF16), wo_ref[...])
    o_ref[...] = x_ref[...] + _rms_normalize(mix, g_ref[...])


def _merge(x2d, y_swa, y_sb, y_mem, gl, w_swa, w_sb, w_mem, w_out, gain):
    t = x2d.shape[0]
    row = lambda i: (i, 0)
    const = lambda i: (0, 0)
    return pl.pallas_call(
        _merge_kernel,
        grid=(t // TM_MERGE,),
        in_specs=[
            pl.BlockSpec((TM_MERGE, D_MODEL), row),
            pl.BlockSpec((TM_MERGE, SWA_Q_W), row),
            pl.BlockSpec((TM_MERGE, SB_W), row),
            pl.BlockSpec((TM_MERGE, MEM_W), row),
            pl.BlockSpec((TM_MERGE, N_BRANCH * D_MODEL), row),
            pl.BlockSpec((SWA_Q_W, D_MODEL), const),
            pl.BlockSpec((SB_W, D_MODEL), const),
            pl.BlockSpec((MEM_W, D_MODEL), const),
            pl.BlockSpec((D_MODEL, D_MODEL), const),
            pl.BlockSpec((1, D_MODEL), const),
        ],
        out_specs=pl.BlockSpec((TM_MERGE, D_MODEL), row),
        out_shape=jax.ShapeDtypeStruct((t, D_MODEL), F32),
        compiler_params=pltpu.CompilerParams(
            dimension_semantics=("arbitrary",), vmem_limit_bytes=VMEM_LIMIT_BYTES),
        name="merge",
    )(x2d, y_swa, y_sb, y_mem, gl, w_swa, w_sb, w_mem, w_out, gain)


def _ffn_kernel(h_ref, gpre_ref, wg_ref, wu_ref, wd_ref, gpost_ref, o_ref):
    h = h_ref[...]
    u = _rms_normalize(h, gpre_ref[...]).astype(BF16)
    acc = None
    for start, width in FF_CHUNKS:
        gate = _dot(u, wg_ref[:, start:start + width])
        up = _dot(u, wu_ref[:, start:start + width])
        act = (gate * jax.nn.sigmoid(gate) * up).astype(BF16)
        part = _dot(act, wd_ref[start:start + width, :])
        acc = part if acc is None else acc + part
    o_ref[...] = h + _rms_normalize(acc, gpost_ref[...])


def _ffn(h2d, g_pre, w_gate, w_up, w_down, g_post):
    t = h2d.shape[0]
    row = lambda i: (i, 0)
    const = lambda i: (0, 0)
    resident = functools.partial(pl.BlockSpec, index_map=const, pipeline_mode=pl.Buffered(1))
    return pl.pallas_call(
        _ffn_kernel,
        grid=(t // TM_FFN,),
        in_specs=[
            pl.BlockSpec((TM_FFN, D_MODEL), row),
            pl.BlockSpec((1, D_MODEL), const),
            resident((D_MODEL, D_FF)),
            resident((D_MODEL, D_FF)),
            resident((D_FF, D_MODEL)),
            pl.BlockSpec((1, D_MODEL), const),
        ],
        out_specs=pl.BlockSpec((TM_FFN, D_MODEL), row),
        out_shape=jax.ShapeDtypeStruct((t, D_MODEL), F32),
        compiler_params=pltpu.CompilerParams(
            dimension_semantics=("arbitrary",), vmem_limit_bytes=VMEM_LIMIT_BYTES),
        name="ffn",
    )(h2d, g_pre, w_gate, w_up, w_down, g_post)


def _prepare_w_in(w_in):
    scale = HEAD_DIM ** -0.5
    assert scale == 0.125
    offs = np.cumsum([0] + [w for _, w in PROJ_PIECES])
    qa = w_in[:, offs[0]:offs[1]].reshape(D_MODEL, SWA_Q_HEADS, HEAD_DIM)
    qa = (qa[:, np.array(SWA_HEAD_ORDER), :] * scale).reshape(D_MODEL, SWA_Q_W)
    qb = w_in[:, offs[3]:offs[4]] * scale
    w = jnp.concatenate([qa, w_in[:, offs[1]:offs[3]], qb, w_in[:, offs[4]:]], axis=1)
    return w.astype(BF16)


def _prepare_w_branch_swa(w):
    w = w.reshape(SWA_Q_HEADS, HEAD_DIM, D_MODEL)[np.array(SWA_HEAD_ORDER)]
    return w.reshape(SWA_Q_W, D_MODEL).astype(BF16)


def kernel(x, mem, ln_mix_pre, ln_mix_post, w_in, swa_sinks, rel_bias, ln_mem, w_mem_kv,
           w_branch_swa, w_branch_sb, w_branch_mem, w_out, ln_ffn_pre, ln_ffn_post,
           w_gate, w_up, w_down):
    batch, seq, d = x.shape
    depth = w_in.shape[0]
    h = x.reshape(batch * seq, d)
    bias = _swa_bias_table(rel_bias)
    for l in range(depth):
        qa, ka, va, qb, kb, vb, qm, gl = _in_proj(h, ln_mix_pre[l][None], _prepare_w_in(w_in[l]))
        y_swa = _swa_attention(qa, ka, va, swa_sinks[l].astype(F32), bias, batch, seq)
        y_sb = _sb_attention(qb, kb, vb, batch, seq)
        mk, mv = _mem_kv(mem, ln_mem[l][None], w_mem_kv[l].astype(BF16))
        y_mem = _mem_attention(qm, mk, mv, batch, seq)
        h = _merge(h, y_swa.reshape(batch * seq, SWA_Q_W), y_sb.reshape(batch * seq, SB_W),
                   y_mem.reshape(batch * seq, MEM_W), gl,
                   _prepare_w_branch_swa(w_branch_swa[l]), w_branch_sb[l].astype(BF16),
                   w_branch_mem[l].astype(BF16), w_out[l].astype(BF16), ln_mix_post[l][None])
        h = _ffn(h, ln_ffn_pre[l][None], w_gate[l].astype(BF16), w_up[l].astype(BF16),
                 w_down[l].astype(BF16), ln_ffn_post[l][None])
    return h.reshape(batch, seq, d)
```

```python
import functools
import math

import numpy as np
import jax
import jax.numpy as jnp
from jax import lax
from jax.experimental import pallas as pl
from jax.experimental.pallas import tpu as pltpu

F32 = jnp.float32
BF16 = jnp.bfloat16

D_MODEL = 1024
BLOCK = 128
EPS = 1e-6
HEAD_DIM = 64
SWA_Q_HEADS = 8
SWA_KV_HEADS = 2
SWA_WINDOW = 128
N_BUCKETS = 32
MAX_DISTANCE = 128
SB_HEADS = 8
MEM_HEADS = 4
MEM_HEAD_DIM = 128
SWA_Q_W = SWA_Q_HEADS * HEAD_DIM
SWA_KV_W = SWA_KV_HEADS * HEAD_DIM
SB_W = SB_HEADS * HEAD_DIM
MEM_W = MEM_HEADS * MEM_HEAD_DIM
N_BRANCH = 3
D_FF = 2816

LANES = 128
VMEM_LIMIT_BYTES = 56 * 1024 * 1024

TM_PROJ = 1024
TM_MERGE = 1024
TM_FFN = 1024
TQ_SB = 256
KB_SB = 256
TQ_MEM = 512
SWA_BLOCKS_PER_STEP = 4
COL_CHUNK = 512
ROW_GROUP = 512
SB_DEAD_CARRY = 128.0
FF_CHUNKS = ((0, 1024), (1024, 1024), (2048, 768))

PROJ_PIECES = (("qa", SWA_Q_W), ("ka", SWA_KV_W), ("va", SWA_KV_W), ("qb", SB_W),
               ("kb", SB_W), ("vb", SB_W), ("qm", MEM_W))
QKV_W = sum(w for _, w in PROJ_PIECES)

SWA_HEAD_ORDER = (0, 4, 1, 5, 2, 6, 3, 7)


def _inv_rms(x):
    return lax.rsqrt(jnp.mean(x * x, axis=-1, keepdims=True) + EPS)


def _rms_normalize(x, gain):
    return x * _inv_rms(x) * gain


def _gained_bf16(x, gain):
    return (x * gain).astype(BF16)


def _half_mask(shape, half):
    lane = lax.broadcasted_iota(jnp.int32, shape, len(shape) - 1)
    return (lane < HEAD_DIM) if half == 0 else (lane >= HEAD_DIM)


def _row_groups(rows):
    return [slice(s, s + ROW_GROUP) for s in range(0, rows, ROW_GROUP)]


def _dot_nt(a, b):
    return lax.dot_general(a, b, (((1,), (1,)), ((), ())), preferred_element_type=F32)


def _dot(a, b):
    return jnp.dot(a, b, preferred_element_type=F32)


def _in_proj_kernel(x_ref, g_ref, w_ref, *out_refs):
    x = x_ref[...]
    u = _gained_bf16(x, g_ref[...])
    inv = _inv_rms(x)
    starts = np.cumsum([0] + [w for _, w in PROJ_PIECES])
    for c0 in range(0, QKV_W, COL_CHUNK):
        c1 = min(c0 + COL_CHUNK, QKV_W)
        y = (_dot(u, w_ref[:, c0:c1]) * inv).astype(BF16)
        for out_ref, p0, p1 in zip(out_refs, starts[:-1], starts[1:]):
            lo, hi = max(c0, p0), min(c1, p1)
            if lo < hi:
                out_ref[:, lo - p0:hi - p0] = y[:, lo - c0:hi - c0]


def _in_proj(x2d, gain, w_bf16):
    t = x2d.shape[0]
    in_w = w_bf16.shape[1]
    out_shape = [jax.ShapeDtypeStruct((t, w), BF16) for _, w in PROJ_PIECES]
    out_specs = [pl.BlockSpec((TM_PROJ, w), lambda i: (i, 0)) for _, w in PROJ_PIECES]
    return pl.pallas_call(
        _in_proj_kernel,
        grid=(t // TM_PROJ,),
        in_specs=[
            pl.BlockSpec((TM_PROJ, D_MODEL), lambda i: (i, 0)),
            pl.BlockSpec((1, D_MODEL), lambda i: (0, 0)),
            pl.BlockSpec((D_MODEL, in_w), lambda i: (0, 0), pipeline_mode=pl.Buffered(1)),
        ],
        out_specs=out_specs,
        out_shape=out_shape,
        compiler_params=pltpu.CompilerParams(
            dimension_semantics=("arbitrary",), vmem_limit_bytes=VMEM_LIMIT_BYTES),
        name="in_proj",
    )(x2d, gain, w_bf16)


def _swa_kernel(sink_ref, q_ref, kp_ref, kc_ref, vp_ref, vc_ref, bias_ref, o_ref, qs_ref):
    i = pl.program_id(1)
    half0 = _half_mask((BLOCK, LANES), 0)
    r = lax.broadcasted_iota(jnp.int32, (BLOCK, BLOCK), 0)
    c = lax.broadcasted_iota(jnp.int32, (BLOCK, BLOCK), 1)
    from_prev = c > r
    missing = jnp.logical_and(from_prev, i == 0)
    rows = SWA_Q_HEADS * BLOCK

    scores, vbands = [], []
    for j in range(SWA_BLOCKS_PER_STEP):
        blk = slice(j * BLOCK, (j + 1) * BLOCK)
        for p in range(SWA_Q_HEADS // 2):
            q2 = q_ref[blk, p * LANES:(p + 1) * LANES]
            zero = jnp.zeros_like(q2)
            base = j * rows + 2 * p * BLOCK
            qs_ref[base:base + BLOCK, :] = jnp.where(half0, q2, zero)
            qs_ref[base + BLOCK:base + 2 * BLOCK, :] = jnp.where(half0, zero, q2)
        if j == 0:
            kband = jnp.concatenate([kp_ref[...], kc_ref[blk, :]], axis=0)
            vband = jnp.concatenate([vp_ref[...], vc_ref[blk, :]], axis=0)
        else:
            band = slice((j - 1) * BLOCK, (j + 1) * BLOCK)
            kband, vband = kc_ref[band, :], vc_ref[band, :]
        scores.append(_dot_nt(qs_ref[j * rows:(j + 1) * rows, :], kband))
        vbands.append(vband)

    for j in range(SWA_BLOCKS_PER_STEP):
        z = scores[j]
        weights, recips = [], []
        for h in range(SWA_Q_HEADS):
            head = SWA_HEAD_ORDER[h]
            zh = z[h * BLOCK:(h + 1) * BLOCK]
            s = jnp.where(from_prev, zh[:, :BLOCK], zh[:, BLOCK:]) + bias_ref[head]
            if j == 0:
                s = jnp.where(missing, -jnp.inf, s)
            sink = sink_ref[head]
            m = jnp.maximum(jnp.max(s, axis=-1, keepdims=True), sink)
            pr = jnp.exp(s - m)
            denom = jnp.sum(pr, axis=-1, keepdims=True) + jnp.exp(sink - m)
            recips.append(1.0 / denom)
            zero = jnp.zeros_like(pr)
            unfolded = jnp.concatenate(
                [jnp.where(from_prev, pr, zero), jnp.where(from_prev, zero, pr)], axis=1)
            weights.append(unfolded.astype(BF16))
        o = _dot(jnp.concatenate(weights, axis=0), vbands[j])
        for p in range(SWA_Q_HEADS // 2):
            o0 = o[(2 * p) * BLOCK:(2 * p + 1) * BLOCK] * recips[2 * p]
            o1 = o[(2 * p + 1) * BLOCK:(2 * p + 2) * BLOCK] * recips[2 * p + 1]
            o_ref[j * BLOCK:(j + 1) * BLOCK, p * LANES:(p + 1) * LANES] = (
                jnp.where(half0, o0, o1).astype(BF16))


def _swa_attention(qa, ka, va, sinks, bias, batch, seq):
    tq = SWA_BLOCKS_PER_STEP * BLOCK
    qa = qa.reshape(batch, seq, SWA_Q_W)
    ka = ka.reshape(batch, seq, SWA_KV_W)
    va = va.reshape(batch, seq, SWA_KV_W)
    cur = lambda b, i: (b, i, 0)
    prev = lambda b, i: (b, jnp.maximum(i * SWA_BLOCKS_PER_STEP - 1, 0), 0)
    return pl.pallas_call(
        _swa_kernel,
        grid=(batch, seq // tq),
        in_specs=[
            pl.BlockSpec(memory_space=pltpu.SMEM),
            pl.BlockSpec((None, tq, SWA_Q_W), cur),
            pl.BlockSpec((None, BLOCK, SWA_KV_W), prev),
            pl.BlockSpec((None, tq, SWA_KV_W), cur),
            pl.BlockSpec((None, BLOCK, SWA_KV_W), prev),
            pl.BlockSpec((None, tq, SWA_KV_W), cur),
            pl.BlockSpec((SWA_Q_HEADS, BLOCK, BLOCK), lambda b, i: (0, 0, 0)),
        ],
        out_specs=pl.BlockSpec((None, tq, SWA_Q_W), cur),
        out_shape=jax.ShapeDtypeStruct((batch, seq, SWA_Q_W), BF16),
        scratch_shapes=[pltpu.VMEM((SWA_BLOCKS_PER_STEP * SWA_Q_HEADS * BLOCK, LANES), BF16)],
        compiler_params=pltpu.CompilerParams(
            dimension_semantics=("arbitrary", "arbitrary"), vmem_limit_bytes=VMEM_LIMIT_BYTES),
        name="swa_attention",
    )(sinks, qa, ka, ka, va, va, bias)


def _t5_bucket(dist):
    max_exact = N_BUCKETS // 2
    d = jnp.maximum(dist, 0)
    df = jnp.maximum(d, 1).astype(F32)
    large = max_exact + (jnp.log(df / max_exact) / math.log(MAX_DISTANCE / max_exact)
                         * (N_BUCKETS - max_exact)).astype(jnp.int32)
    large = jnp.minimum(large, N_BUCKETS - 1)
    return jnp.where(d < max_exact, d, large)


def _swa_bias_table(rel_bias):
    r = jnp.arange(BLOCK)[:, None]
    c = jnp.arange(BLOCK)[None, :]
    dist = jnp.where(c > r, r + BLOCK - c, r - c)
    assert SWA_WINDOW == BLOCK
    bucket = _t5_bucket(dist)
    rb = rel_bias.astype(F32)
    bias = jnp.zeros((SWA_Q_HEADS, BLOCK, BLOCK), F32)
    for b in range(N_BUCKETS):
        bias = jnp.where((bucket == b)[None], rb[b][:, None, None], bias)
    return bias


def _neg_abs(x):
    bits = lax.bitcast_convert_type(x, jnp.int32) | jnp.int32(-2 ** 31)
    return lax.bitcast_convert_type(bits, F32)


def _sb_kernel(q_ref, k_ref, v_ref, o_ref, qs_ref, acc_ref, carry_ref):
    i = pl.program_id(1)
    n_pairs = SB_HEADS // 2
    half0 = _half_mask((TQ_SB, LANES), 0)
    r = lax.broadcasted_iota(jnp.int32, (KB_SB, KB_SB), 0)
    c = lax.broadcasted_iota(jnp.int32, (KB_SB, KB_SB), 1)
    suffix = jnp.where(r > c, 1.0, 0.0).astype(BF16)
    rr = lax.broadcasted_iota(jnp.int32, (2 * TQ_SB, KB_SB), 0)
    cc = lax.broadcasted_iota(jnp.int32, (2 * TQ_SB, KB_SB), 1)
    causal = cc < jnp.where(rr >= TQ_SB, rr - TQ_SB, rr)

    for p in range(n_pairs):
        q2 = q_ref[:, p * LANES:(p + 1) * LANES]
        zero = jnp.zeros_like(q2)
        qs_ref[p, :TQ_SB, :] = jnp.where(half0, q2, zero)
        qs_ref[p, TQ_SB:, :] = jnp.where(half0, zero, q2)
    lanes = [slice(p * LANES, (p + 1) * LANES) for p in range(n_pairs)]
    rows = 2 * TQ_SB

    def score_phase(j, diagonal):
        start = pl.multiple_of(j * KB_SB, KB_SB)
        zs = [_dot_nt(qs_ref[p], k_ref[pl.ds(start, KB_SB), lanes[p]]) for p in range(n_pairs)]
        parts = []
        for z in zs:
            sp = jnp.maximum(z, 0.0) + jnp.log(1.0 + jnp.exp(_neg_abs(z)))
            lz = z - sp
            if diagonal:
                sp = jnp.where(causal, sp, 0.0)
                lz = jnp.where(causal, lz, -jnp.inf)
            parts.append((sp.astype(BF16), lz, sp[:, 0:1]))
        return start, parts

    def value_phase(p, start, part, within, carry):
        _, lz, first = part
        a = jnp.exp(lz - (within if carry is None else within + carry))
        pv = _dot(a.astype(BF16), v_ref[pl.ds(start, KB_SB), lanes[p]])
        total = within[:, 0:1] + first
        if carry is not None:
            total = carry + total
        return jnp.where(half0, pv[:TQ_SB], pv[TQ_SB:]), total

    def visit_first(diagonal_only):
        start_d, parts_d = score_phase(i, True)
        sps = [part[0] for part in parts_d]
        if not diagonal_only:
            start_o, parts_o = score_phase(i - 1, False)
            sps += [part[0] for part in parts_o]
        within_all = _dot(jnp.concatenate(sps, axis=0), suffix)
        for p in range(n_pairs):
            upd, total = value_phase(p, start_d, parts_d[p], within_all[p * rows:(p + 1) * rows], None)
            if not diagonal_only:
                q = n_pairs + p
                upd_o, total = value_phase(p, start_o, parts_o[p],
                                           within_all[q * rows:(q + 1) * rows], total)
                upd = upd + upd_o
            acc_ref[p] = upd
            carry_ref[p] = total

    def visit_next(j):
        start, parts = score_phase(j, False)
        within_all = _dot(jnp.concatenate([part[0] for part in parts], axis=0), suffix)
        for p in range(n_pairs):
            upd, total = value_phase(p, start, parts[p], within_all[p * rows:(p + 1) * rows],
                                     carry_ref[p])
            acc_ref[p] += upd
            carry_ref[p] = total

    def all_dead():
        least = carry_ref[0]
        for p in range(1, n_pairs):
            least = jnp.minimum(least, carry_ref[p])
        return (jnp.min(least) >= SB_DEAD_CARRY).astype(jnp.int32)

    pl.when(i == 0)(functools.partial(visit_first, True))
    pl.when(i > 0)(functools.partial(visit_first, False))

    def live(state):
        n, dead = state
        return jnp.logical_and(n < i, dead == 0)

    def body(state):
        n, _ = state
        visit_next(i - 1 - n)
        return n + 1, all_dead()

    lax.while_loop(live, body, (jnp.int32(1), all_dead()))
    for p in range(n_pairs):
        o_ref[:, p * LANES:(p + 1) * LANES] = acc_ref[p].astype(BF16)


def _sb_attention(qb, kb, vb, batch, seq):
    qb = qb.reshape(batch, seq, SB_W)
    kb = kb.reshape(batch, seq, SB_W)
    vb = vb.reshape(batch, seq, SB_W)
    assert TQ_SB == KB_SB
    q_map = lambda b, i: (b, i, 0)
    kv_map = lambda b, i: (b, 0, 0)
    return pl.pallas_call(
        _sb_kernel,
        grid=(batch, seq // TQ_SB),
        in_specs=[
            pl.BlockSpec((None, TQ_SB, SB_W), q_map),
            pl.BlockSpec((None, seq, SB_W), kv_map),
            pl.BlockSpec((None, seq, SB_W), kv_map),
        ],
        out_specs=pl.BlockSpec((None, TQ_SB, SB_W), q_map),
        out_shape=jax.ShapeDtypeStruct((batch, seq, SB_W), BF16),
        scratch_shapes=[
            pltpu.VMEM((SB_HEADS // 2, 2 * TQ_SB, LANES), BF16),
            pltpu.VMEM((SB_HEADS // 2, TQ_SB, LANES), F32),
            pltpu.VMEM((SB_HEADS // 2, 2 * TQ_SB, 1), F32),
        ],
        compiler_params=pltpu.CompilerParams(
            dimension_semantics=("arbitrary", "arbitrary"),
            vmem_limit_bytes=VMEM_LIMIT_BYTES),
        name="sb_attention",
    )(qb, kb, vb)


def _mem_kv_kernel(m_ref, g_ref, w_ref, k_ref, v_ref):
    u = _rms_normalize(m_ref[...], g_ref[...]).astype(BF16)
    k_ref[...] = _dot(u, w_ref[:, :MEM_W]).astype(BF16)
    v_ref[...] = _dot(u, w_ref[:, MEM_W:]).astype(BF16)


def _mem_kv(mem, gain, w_bf16):
    batch, mem_len, _ = mem.shape
    blk = lambda b: (b, 0, 0)
    return pl.pallas_call(
        _mem_kv_kernel,
        grid=(batch,),
        in_specs=[
            pl.BlockSpec((None, mem_len, D_MODEL), blk),
            pl.BlockSpec((1, D_MODEL), lambda b: (0, 0)),
            pl.BlockSpec((D_MODEL, 2 * MEM_W), lambda b: (0, 0)),
        ],
        out_specs=[pl.BlockSpec((None, mem_len, MEM_W), blk)] * 2,
        out_shape=[jax.ShapeDtypeStruct((batch, mem_len, MEM_W), BF16)] * 2,
        compiler_params=pltpu.CompilerParams(
            dimension_semantics=("arbitrary",), vmem_limit_bytes=VMEM_LIMIT_BYTES),
        name="mem_kv",
    )(mem, gain, w_bf16)


def _mem_attn_kernel(q_ref, k_ref, v_ref, o_ref):
    c = (MEM_HEAD_DIM ** -0.5) * math.log2(math.e)
    heads = [slice(h * MEM_HEAD_DIM, (h + 1) * MEM_HEAD_DIM) for h in range(MEM_HEADS)]
    zs = [_dot_nt(q_ref[:, sl], k_ref[:, sl]) for sl in heads]
    for sl, z in zip(heads, zs):
        pr = jnp.exp2((z - jnp.max(z, axis=-1, keepdims=True)) * c)
        recip = 1.0 / jnp.sum(pr, axis=-1, keepdims=True)
        o_ref[:, sl] = (_dot(pr.astype(BF16), v_ref[:, sl]) * recip).astype(BF16)


def _mem_attention(qm, mk, mv, batch, seq):
    qm = qm.reshape(batch, seq, MEM_W)
    mem_len = mk.shape[1]
    q_map = lambda b, i: (b, i, 0)
    kv_map = lambda b, i: (b, 0, 0)
    return pl.pallas_call(
        _mem_attn_kernel,
        grid=(batch, seq // TQ_MEM),
        in_specs=[
            pl.BlockSpec((None, TQ_MEM, MEM_W), q_map),
            pl.BlockSpec((None, mem_len, MEM_W), kv_map),
            pl.BlockSpec((None, mem_len, MEM_W), kv_map),
        ],
        out_specs=pl.BlockSpec((None, TQ_MEM, MEM_W), q_map),
        out_shape=jax.ShapeDtypeStruct((batch, seq, MEM_W), BF16),
        compiler_params=pltpu.CompilerParams(
            dimension_semantics=("arbitrary", "arbitrary"), vmem_limit_bytes=VMEM_LIMIT_BYTES),
        name="mem_attention",
    )(qm, mk, mv)


def _merge_kernel(x_ref, gpre_ref, ya_ref, yb_ref, ym_ref, wgl_ref, wa_ref, wb_ref, wm_ref,
                  wo_ref, gpost_ref, o_ref):
    x = x_ref[...]
    u = _gained_bf16(x, gpre_ref[...])
    inv = _inv_rms(x)
    merged = None
    for n, (y_ref, w_ref) in enumerate(((ya_ref, wa_ref), (yb_ref, wb_ref), (ym_ref, wm_ref))):
        gate = jax.nn.sigmoid(_dot(u, wgl_ref[:, n * D_MODEL:(n + 1) * D_MODEL]) * inv)
        term = gate * _dot(y_ref[...], w_ref[...])
        merged = term if merged is None else merged + term
    mix = _dot(merged.astype(BF16), wo_ref[...])
    o_ref[...] = x + _rms_normalize(mix, gpost_ref[...])


def _merge(x2d, g_pre, y_swa, y_sb, y_mem, w_gl, w_swa, w_sb, w_mem, w_out, g_post):
    t = x2d.shape[0]
    row = lambda i: (i, 0)
    const = lambda i: (0, 0)
    resident = functools.partial(pl.BlockSpec, index_map=const, pipeline_mode=pl.Buffered(1))
    return pl.pallas_call(
        _merge_kernel,
        grid=(t // TM_MERGE,),
        in_specs=[
            pl.BlockSpec((TM_MERGE, D_MODEL), row),
            pl.BlockSpec((1, D_MODEL), const),
            pl.BlockSpec((TM_MERGE, SWA_Q_W), row),
            pl.BlockSpec((TM_MERGE, SB_W), row),
            pl.BlockSpec((TM_MERGE, MEM_W), row),
            resident((D_MODEL, N_BRANCH * D_MODEL)),
            resident((SWA_Q_W, D_MODEL)),
            resident((SB_W, D_MODEL)),
            resident((MEM_W, D_MODEL)),
            resident((D_MODEL, D_MODEL)),
            pl.BlockSpec((1, D_MODEL), const),
        ],
        out_specs=pl.BlockSpec((TM_MERGE, D_MODEL), row),
        out_shape=jax.ShapeDtypeStruct((t, D_MODEL), F32),
        compiler_params=pltpu.CompilerParams(
            dimension_semantics=("arbitrary",), vmem_limit_bytes=VMEM_LIMIT_BYTES),
        name="merge",
    )(x2d, g_pre, y_swa, y_sb, y_mem, w_gl, w_swa, w_sb, w_mem, w_out, g_post)


def _ffn_kernel(h_ref, gpre_ref, wg_ref, wu_ref, wd_ref, gpost_ref, o_ref):
    for rows in _row_groups(TM_FFN):
        h = h_ref[rows, :]
        u = _gained_bf16(h, gpre_ref[...])
        inv = _inv_rms(h)
        acc = None
        for start, width in FF_CHUNKS:
            gate = _dot(u, wg_ref[:, start:start + width]) * inv
            up = _dot(u, wu_ref[:, start:start + width]) * inv
            act = (gate * jax.nn.sigmoid(gate) * up).astype(BF16)
            part = _dot(act, wd_ref[start:start + width, :])
            acc = part if acc is None else acc + part
        o_ref[rows, :] = h + _rms_normalize(acc, gpost_ref[...])


def _ffn(h2d, g_pre, w_gate, w_up, w_down, g_post):
    t = h2d.shape[0]
    row = lambda i: (i, 0)
    const = lambda i: (0, 0)
    resident = functools.partial(pl.BlockSpec, index_map=const, pipeline_mode=pl.Buffered(1))
    return pl.pallas_call(
        _ffn_kernel,
        grid=(t // TM_FFN,),
        in_specs=[
            pl.BlockSpec((TM_FFN, D_MODEL), row),
            pl.BlockSpec((1, D_MODEL), const),
            resident((D_MODEL, D_FF)),
            resident((D_MODEL, D_FF)),
            resident((D_FF, D_MODEL)),
            pl.BlockSpec((1, D_MODEL), const),
        ],
        out_specs=pl.BlockSpec((TM_FFN, D_MODEL), row),
        out_shape=jax.ShapeDtypeStruct((t, D_MODEL), F32),
        compiler_params=pltpu.CompilerParams(
            dimension_semantics=("arbitrary",), vmem_limit_bytes=VMEM_LIMIT_BYTES),
        name="ffn",
    )(h2d, g_pre, w_gate, w_up, w_down, g_post)


def _prepare_w_in(w_in):
    scale = HEAD_DIM ** -0.5
    assert scale == 0.125
    offs = np.cumsum([0] + [w for _, w in PROJ_PIECES])
    qa = w_in[:, offs[0]:offs[1]].reshape(D_MODEL, SWA_Q_HEADS, HEAD_DIM)
    qa = (qa[:, np.array(SWA_HEAD_ORDER), :] * scale).reshape(D_MODEL, SWA_Q_W)
    qb = w_in[:, offs[3]:offs[4]] * scale
    w_qkv = jnp.concatenate([qa, w_in[:, offs[1]:offs[3]], qb, w_in[:, offs[4]:QKV_W]], axis=1)
    return w_qkv.astype(BF16), w_in[:, QKV_W:].astype(BF16)


def _prepare_w_branch_swa(w):
    w = w.reshape(SWA_Q_HEADS, HEAD_DIM, D_MODEL)[np.array(SWA_HEAD_ORDER)]
    return w.reshape(SWA_Q_W, D_MODEL).astype(BF16)


def kernel(x, mem, ln_mix_pre, ln_mix_post, w_in, swa_sinks, rel_bias, ln_mem, w_mem_kv,
           w_branch_swa, w_branch_sb, w_branch_mem, w_out, ln_ffn_pre, ln_ffn_post,
           w_gate, w_up, w_down):
    batch, seq, d = x.shape
    depth = w_in.shape[0]
    h = x.reshape(batch * seq, d)
    bias = _swa_bias_table(rel_bias)
    for l in range(depth):
        w_qkv, w_gl = _prepare_w_in(w_in[l])
        qa, ka, va, qb, kb, vb, qm = _in_proj(h, ln_mix_pre[l][None], w_qkv)
        y_swa = _swa_attention(qa, ka, va, swa_sinks[l].astype(F32), bias, batch, seq)
        y_sb = _sb_attention(qb, kb, vb, batch, seq)
        mk, mv = _mem_kv(mem, ln_mem[l][None], w_mem_kv[l].astype(BF16))
        y_mem = _mem_attention(qm, mk, mv, batch, seq)
        h = _merge(h, ln_mix_pre[l][None], y_swa.reshape(batch * seq, SWA_Q_W),
                   y_sb.reshape(batch * seq, SB_W), y_mem.reshape(batch * seq, MEM_W), w_gl,
                   _prepare_w_branch_swa(w_branch_swa[l]), w_branch_sb[l].astype(BF16),
                   w_branch_mem[l].astype(BF16), w_out[l].astype(BF16), ln_mix_post[l][None])
        h = _ffn(h, ln_ffn_pre[l][None], w_gate[l].astype(BF16), w_up[l].astype(BF16),
                 w_down[l].astype(BF16), ln_ffn_post[l][None])
    return h.reshape(batch, seq, d)
```

```python
import functools
import math

import numpy as np
import jax
import jax.numpy as jnp
from jax import lax
from jax.experimental import pallas as pl
from jax.experimental.pallas import tpu as pltpu

F32 = jnp.float32
BF16 = jnp.bfloat16

D_MODEL = 1024
BLOCK = 128
EPS = 1e-6
HEAD_DIM = 64
SWA_Q_HEADS = 8
SWA_KV_HEADS = 2
SWA_WINDOW = 128
N_BUCKETS = 32
MAX_DISTANCE = 128
SB_HEADS = 8
MEM_HEADS = 4
MEM_HEAD_DIM = 128
SWA_Q_W = SWA_Q_HEADS * HEAD_DIM
SWA_KV_W = SWA_KV_HEADS * HEAD_DIM
SB_W = SB_HEADS * HEAD_DIM
MEM_W = MEM_HEADS * MEM_HEAD_DIM
N_BRANCH = 3
D_FF = 2816

LANES = 128
VMEM_LIMIT_BYTES = 56 * 1024 * 1024

TM_PROJ = 1024
TM_MERGE = 1024
TM_FFN = 1024
TQ_SB = 256
KB_SB = 256
TQ_MEM = 512
SWA_BLOCKS_PER_STEP = 4
COL_CHUNK = 512
ROW_GROUP = 512
SB_DEAD_CARRY = 96.0
SB_TOP_ROWS = 160
FF_CHUNKS = ((0, 1024), (1024, 1024), (2048, 768))

PROJ_PIECES = (("qa", SWA_Q_W), ("ka", SWA_KV_W), ("va", SWA_KV_W), ("qb", SB_W),
               ("kb", SB_W), ("vb", SB_W), ("qm", MEM_W))
QKV_W = sum(w for _, w in PROJ_PIECES)

SWA_HEAD_ORDER = (0, 4, 1, 5, 2, 6, 3, 7)


def _inv_rms(x):
    return lax.rsqrt(jnp.mean(x * x, axis=-1, keepdims=True) + EPS)


def _rms_normalize(x, gain):
    return x * _inv_rms(x) * gain


def _gained_bf16(x, gain):
    return (x * gain).astype(BF16)


def _half_mask(shape, half):
    lane = lax.broadcasted_iota(jnp.int32, shape, len(shape) - 1)
    return (lane < HEAD_DIM) if half == 0 else (lane >= HEAD_DIM)


def _row_groups(rows):
    return [slice(s, s + ROW_GROUP) for s in range(0, rows, ROW_GROUP)]


def _dot_nt(a, b):
    return lax.dot_general(a, b, (((1,), (1,)), ((), ())), preferred_element_type=F32)


def _dot(a, b):
    return jnp.dot(a, b, preferred_element_type=F32)


def _in_proj_kernel(x_ref, g_ref, w_ref, *out_refs):
    x = x_ref[...]
    u = _gained_bf16(x, g_ref[...])
    inv = _inv_rms(x)
    starts = np.cumsum([0] + [w for _, w in PROJ_PIECES])
    for c0 in range(0, QKV_W, COL_CHUNK):
        c1 = min(c0 + COL_CHUNK, QKV_W)
        y = (_dot(u, w_ref[:, c0:c1]) * inv).astype(BF16)
        for out_ref, p0, p1 in zip(out_refs, starts[:-1], starts[1:]):
            lo, hi = max(c0, p0), min(c1, p1)
            if lo < hi:
                out_ref[:, lo - p0:hi - p0] = y[:, lo - c0:hi - c0]


def _in_proj(x2d, gain, w_bf16):
    t = x2d.shape[0]
    in_w = w_bf16.shape[1]
    out_shape = [jax.ShapeDtypeStruct((t, w), BF16) for _, w in PROJ_PIECES]
    out_specs = [pl.BlockSpec((TM_PROJ, w), lambda i: (i, 0)) for _, w in PROJ_PIECES]
    return pl.pallas_call(
        _in_proj_kernel,
        grid=(t // TM_PROJ,),
        in_specs=[
            pl.BlockSpec((TM_PROJ, D_MODEL), lambda i: (i, 0)),
            pl.BlockSpec((1, D_MODEL), lambda i: (0, 0)),
            pl.BlockSpec((D_MODEL, in_w), lambda i: (0, 0), pipeline_mode=pl.Buffered(1)),
        ],
        out_specs=out_specs,
        out_shape=out_shape,
        compiler_params=pltpu.CompilerParams(
            dimension_semantics=("arbitrary",), vmem_limit_bytes=VMEM_LIMIT_BYTES),
        name="in_proj",
    )(x2d, gain, w_bf16)


def _swa_kernel(sink_ref, q_ref, kp_ref, kc_ref, vp_ref, vc_ref, bias_ref, o_ref, qs_ref):
    i = pl.program_id(1)
    half0 = _half_mask((BLOCK, LANES), 0)
    r = lax.broadcasted_iota(jnp.int32, (BLOCK, BLOCK), 0)
    c = lax.broadcasted_iota(jnp.int32, (BLOCK, BLOCK), 1)
    from_prev = c > r
    missing = jnp.logical_and(from_prev, i == 0)
    rows = SWA_Q_HEADS * BLOCK

    scores, vbands = [], []
    for j in range(SWA_BLOCKS_PER_STEP):
        blk = slice(j * BLOCK, (j + 1) * BLOCK)
        for p in range(SWA_Q_HEADS // 2):
            q2 = q_ref[blk, p * LANES:(p + 1) * LANES]
            zero = jnp.zeros_like(q2)
            base = j * rows + 2 * p * BLOCK
            qs_ref[base:base + BLOCK, :] = jnp.where(half0, q2, zero)
            qs_ref[base + BLOCK:base + 2 * BLOCK, :] = jnp.where(half0, zero, q2)
        if j == 0:
            kband = jnp.concatenate([kp_ref[...], kc_ref[blk, :]], axis=0)
            vband = jnp.concatenate([vp_ref[...], vc_ref[blk, :]], axis=0)
        else:
            band = slice((j - 1) * BLOCK, (j + 1) * BLOCK)
            kband, vband = kc_ref[band, :], vc_ref[band, :]
        scores.append(_dot_nt(qs_ref[j * rows:(j + 1) * rows, :], kband))
        vbands.append(vband)

    for j in range(SWA_BLOCKS_PER_STEP):
        z = scores[j]
        weights, recips = [], []
        for h in range(SWA_Q_HEADS):
            head = SWA_HEAD_ORDER[h]
            zh = z[h * BLOCK:(h + 1) * BLOCK]
            s = jnp.where(from_prev, zh[:, :BLOCK], zh[:, BLOCK:]) + bias_ref[head]
            if j == 0:
                s = jnp.where(missing, -jnp.inf, s)
            sink = sink_ref[head]
            m = jnp.maximum(jnp.max(s, axis=-1, keepdims=True), sink)
            pr = jnp.exp(s - m)
            denom = jnp.sum(pr, axis=-1, keepdims=True) + jnp.exp(sink - m)
            recips.append(1.0 / denom)
            zero = jnp.zeros_like(pr)
            unfolded = jnp.concatenate(
                [jnp.where(from_prev, pr, zero), jnp.where(from_prev, zero, pr)], axis=1)
            weights.append(unfolded.astype(BF16))
        o = _dot(jnp.concatenate(weights, axis=0), vbands[j])
        for p in range(SWA_Q_HEADS // 2):
            o0 = o[(2 * p) * BLOCK:(2 * p + 1) * BLOCK] * recips[2 * p]
            o1 = o[(2 * p + 1) * BLOCK:(2 * p + 2) * BLOCK] * recips[2 * p + 1]
            o_ref[j * BLOCK:(j + 1) * BLOCK, p * LANES:(p + 1) * LANES] = (
                jnp.where(half0, o0, o1).astype(BF16))


def _swa_attention(qa, ka, va, sinks, bias, batch, seq):
    tq = SWA_BLOCKS_PER_STEP * BLOCK
    qa = qa.reshape(batch, seq, SWA_Q_W)
    ka = ka.reshape(batch, seq, SWA_KV_W)
    va = va.reshape(batch, seq, SWA_KV_W)
    cur = lambda b, i: (b, i, 0)
    prev = lambda b, i: (b, jnp.maximum(i * SWA_BLOCKS_PER_STEP - 1, 0), 0)
    return pl.pallas_call(
        _swa_kernel,
        grid=(batch, seq // tq),
        in_specs=[
            pl.BlockSpec(memory_space=pltpu.SMEM),
            pl.BlockSpec((None, tq, SWA_Q_W), cur),
            pl.BlockSpec((None, BLOCK, SWA_KV_W), prev),
            pl.BlockSpec((None, tq, SWA_KV_W), cur),
            pl.BlockSpec((None, BLOCK, SWA_KV_W), prev),
            pl.BlockSpec((None, tq, SWA_KV_W), cur),
            pl.BlockSpec((SWA_Q_HEADS, BLOCK, BLOCK), lambda b, i: (0, 0, 0)),
        ],
        out_specs=pl.BlockSpec((None, tq, SWA_Q_W), cur),
        out_shape=jax.ShapeDtypeStruct((batch, seq, SWA_Q_W), BF16),
        scratch_shapes=[pltpu.VMEM((SWA_BLOCKS_PER_STEP * SWA_Q_HEADS * BLOCK, LANES), BF16)],
        compiler_params=pltpu.CompilerParams(
            dimension_semantics=("arbitrary", "arbitrary"), vmem_limit_bytes=VMEM_LIMIT_BYTES),
        name="swa_attention",
    )(sinks, qa, ka, ka, va, va, bias)


def _t5_bucket(dist):
    max_exact = N_BUCKETS // 2
    d = jnp.maximum(dist, 0)
    df = jnp.maximum(d, 1).astype(F32)
    large = max_exact + (jnp.log(df / max_exact) / math.log(MAX_DISTANCE / max_exact)
                         * (N_BUCKETS - max_exact)).astype(jnp.int32)
    large = jnp.minimum(large, N_BUCKETS - 1)
    return jnp.where(d < max_exact, d, large)


def _swa_bias_table(rel_bias):
    r = jnp.arange(BLOCK)[:, None]
    c = jnp.arange(BLOCK)[None, :]
    dist = jnp.where(c > r, r + BLOCK - c, r - c)
    assert SWA_WINDOW == BLOCK
    bucket = _t5_bucket(dist)
    rb = rel_bias.astype(F32)
    bias = jnp.zeros((SWA_Q_HEADS, BLOCK, BLOCK), F32)
    for b in range(N_BUCKETS):
        bias = jnp.where((bucket == b)[None], rb[b][:, None, None], bias)
    return bias


def _neg_abs(x):
    bits = lax.bitcast_convert_type(x, jnp.int32) | jnp.int32(-2 ** 31)
    return lax.bitcast_convert_type(bits, F32)


def _sb_kernel(q_ref, k_ref, v_ref, o_ref, qs_ref, acc_ref, carry_ref):
    i = pl.program_id(1)
    n_pairs = SB_HEADS // 2
    top, rest = SB_TOP_ROWS, TQ_SB - SB_TOP_ROWS
    rows = 2 * TQ_SB
    half0_top = _half_mask((top, LANES), 0)
    half0_rest = _half_mask((rest, LANES), 0)
    r = lax.broadcasted_iota(jnp.int32, (KB_SB, KB_SB), 0)
    c = lax.broadcasted_iota(jnp.int32, (KB_SB, KB_SB), 1)
    suffix = jnp.where(r > c, 1.0, 0.0).astype(BF16)
    rr = lax.broadcasted_iota(jnp.int32, (rows, KB_SB), 0)
    cc = lax.broadcasted_iota(jnp.int32, (rows, KB_SB), 1)
    tile_row = jnp.where(rr < top, rr, jnp.where(rr < 2 * top + rest, rr - top, rr - top - rest))
    causal = cc < tile_row

    for p in range(n_pairs):
        q2 = q_ref[:, p * LANES:(p + 1) * LANES]
        q_top, q_rest = q2[:top], q2[top:]
        qs_ref[p, 0:top, :] = jnp.where(half0_top, q_top, jnp.zeros_like(q_top))
        qs_ref[p, top:2 * top, :] = jnp.where(half0_top, jnp.zeros_like(q_top), q_top)
        qs_ref[p, 2 * top:2 * top + rest, :] = jnp.where(half0_rest, q_rest, jnp.zeros_like(q_rest))
        qs_ref[p, 2 * top + rest:, :] = jnp.where(half0_rest, jnp.zeros_like(q_rest), q_rest)
    lanes = [slice(p * LANES, (p + 1) * LANES) for p in range(n_pairs)]

    def visit(j, diagonal, top_only):
        n = 2 * top if top_only else rows
        start = pl.multiple_of(j * KB_SB, KB_SB)
        zs = [_dot_nt(qs_ref[p, 0:n, :], k_ref[pl.ds(start, KB_SB), lanes[p]])
              for p in range(n_pairs)]
        sps, lzs, firsts = [], [], []
        for z in zs:
            sp = jnp.maximum(z, 0.0) + jnp.log(1.0 + jnp.exp(_neg_abs(z)))
            lz = z - sp
            if diagonal:
                sp = jnp.where(causal, sp, 0.0)
                lz = jnp.where(causal, lz, -jnp.inf)
            sps.append(sp.astype(BF16))
            lzs.append(lz)
            firsts.append(sp[:, 0:1])
        within_all = _dot(jnp.concatenate(sps, axis=0), suffix)
        for p in range(n_pairs):
            within = within_all[p * n:(p + 1) * n]
            total = within[:, 0:1] + firsts[p]
            if diagonal:
                a = jnp.exp(lzs[p] - within)
            else:
                carry = carry_ref[p, 0:n, :]
                a = jnp.exp(lzs[p] - (within + carry))
                total = carry + total
            pv = _dot(a.astype(BF16), v_ref[pl.ds(start, KB_SB), lanes[p]])
            upd_top = jnp.where(half0_top, pv[0:top], pv[top:2 * top])
            if top_only:
                acc_ref[p, 0:top, :] += upd_top
            else:
                upd_rest = jnp.where(half0_rest, pv[2 * top:2 * top + rest], pv[2 * top + rest:])
                upd = jnp.concatenate([upd_top, upd_rest], axis=0)
                if diagonal:
                    acc_ref[p] = upd
                else:
                    acc_ref[p] += upd
            carry_ref[p, 0:n, :] = total

    def dead(lo, hi):
        least = carry_ref[0, lo:hi, :]
        for p in range(1, n_pairs):
            least = jnp.minimum(least, carry_ref[p, lo:hi, :])
        return (jnp.min(least) >= SB_DEAD_CARRY).astype(jnp.int32)

    visit(i, True, False)

    def live(state):
        n, done = state
        return jnp.logical_and(n < i, done == 0)

    def full_visit(state):
        n, _ = state
        visit(i - 1 - n, False, False)
        return n + 1, dead(2 * top, rows)

    def top_visit(state):
        n, _ = state
        visit(i - 1 - n, False, True)
        return n + 1, dead(0, 2 * top)

    n, _ = lax.while_loop(live, full_visit, (jnp.int32(0), dead(2 * top, rows)))
    lax.while_loop(live, top_visit, (n, dead(0, 2 * top)))
    for p in range(n_pairs):
        o_ref[:, p * LANES:(p + 1) * LANES] = acc_ref[p].astype(BF16)


def _sb_attention(qb, kb, vb, batch, seq):
    qb = qb.reshape(batch, seq, SB_W)
    kb = kb.reshape(batch, seq, SB_W)
    vb = vb.reshape(batch, seq, SB_W)
    assert TQ_SB == KB_SB
    q_map = lambda b, i: (b, i, 0)
    kv_map = lambda b, i: (b, 0, 0)
    return pl.pallas_call(
        _sb_kernel,
        grid=(batch, seq // TQ_SB),
        in_specs=[
            pl.BlockSpec((None, TQ_SB, SB_W), q_map),
            pl.BlockSpec((None, seq, SB_W), kv_map),
            pl.BlockSpec((None, seq, SB_W), kv_map),
        ],
        out_specs=pl.BlockSpec((None, TQ_SB, SB_W), q_map),
        out_shape=jax.ShapeDtypeStruct((batch, seq, SB_W), BF16),
        scratch_shapes=[
            pltpu.VMEM((SB_HEADS // 2, 2 * TQ_SB, LANES), BF16),
            pltpu.VMEM((SB_HEADS // 2, TQ_SB, LANES), F32),
            pltpu.VMEM((SB_HEADS // 2, 2 * TQ_SB, 1), F32),
        ],
        compiler_params=pltpu.CompilerParams(
            dimension_semantics=("arbitrary", "arbitrary"),
            vmem_limit_bytes=VMEM_LIMIT_BYTES),
        name="sb_attention",
    )(qb, kb, vb)


def _mem_kv_kernel(m_ref, g_ref, w_ref, k_ref, v_ref):
    u = _rms_normalize(m_ref[...], g_ref[...]).astype(BF16)
    k_ref[...] = _dot(u, w_ref[:, :MEM_W]).astype(BF16)
    v_ref[...] = _dot(u, w_ref[:, MEM_W:]).astype(BF16)


def _mem_kv(mem, gain, w_bf16):
    batch, mem_len, _ = mem.shape
    blk = lambda b: (b, 0, 0)
    return pl.pallas_call(
        _mem_kv_kernel,
        grid=(batch,),
        in_specs=[
            pl.BlockSpec((None, mem_len, D_MODEL), blk),
            pl.BlockSpec((1, D_MODEL), lambda b: (0, 0)),
            pl.BlockSpec((D_MODEL, 2 * MEM_W), lambda b: (0, 0)),
        ],
        out_specs=[pl.BlockSpec((None, mem_len, MEM_W), blk)] * 2,
        out_shape=[jax.ShapeDtypeStruct((batch, mem_len, MEM_W), BF16)] * 2,
        compiler_params=pltpu.CompilerParams(
            dimension_semantics=("arbitrary",), vmem_limit_bytes=VMEM_LIMIT_BYTES),
        name="mem_kv",
    )(mem, gain, w_bf16)


def _mem_attn_kernel(q_ref, k_ref, v_ref, o_ref):
    c = (MEM_HEAD_DIM ** -0.5) * math.log2(math.e)
    heads = [slice(h * MEM_HEAD_DIM, (h + 1) * MEM_HEAD_DIM) for h in range(MEM_HEADS)]
    zs = [_dot_nt(q_ref[:, sl], k_ref[:, sl]) for sl in heads]
    for sl, z in zip(heads, zs):
        pr = jnp.exp2((z - jnp.max(z, axis=-1, keepdims=True)) * c)
        recip = 1.0 / jnp.sum(pr, axis=-1, keepdims=True)
        o_ref[:, sl] = (_dot(pr.astype(BF16), v_ref[:, sl]) * recip).astype(BF16)


def _mem_attention(qm, mk, mv, batch, seq):
    qm = qm.reshape(batch, seq, MEM_W)
    mem_len = mk.shape[1]
    q_map = lambda b, i: (b, i, 0)
    kv_map = lambda b, i: (b, 0, 0)
    return pl.pallas_call(
        _mem_attn_kernel,
        grid=(batch, seq // TQ_MEM),
        in_specs=[
            pl.BlockSpec((None, TQ_MEM, MEM_W), q_map),
            pl.BlockSpec((None, mem_len, MEM_W), kv_map),
            pl.BlockSpec((None, mem_len, MEM_W), kv_map),
        ],
        out_specs=pl.BlockSpec((None, TQ_MEM, MEM_W), q_map),
        out_shape=jax.ShapeDtypeStruct((batch, seq, MEM_W), BF16),
        compiler_params=pltpu.CompilerParams(
            dimension_semantics=("arbitrary", "arbitrary"), vmem_limit_bytes=VMEM_LIMIT_BYTES),
        name="mem_attention",
    )(qm, mk, mv)


def _merge_kernel(x_ref, gpre_ref, ya_ref, yb_ref, ym_ref, wgl_ref, wa_ref, wb_ref, wm_ref,
                  wo_ref, gpost_ref, o_ref):
    for rows in _row_groups(TM_MERGE):
        x = x_ref[rows, :]
        u = _gained_bf16(x, gpre_ref[...])
        inv = _inv_rms(x)
        merged = None
        for n, (y_ref, w_ref) in enumerate(((ya_ref, wa_ref), (yb_ref, wb_ref), (ym_ref, wm_ref))):
            gate = jax.nn.sigmoid(_dot(u, wgl_ref[:, n * D_MODEL:(n + 1) * D_MODEL]) * inv)
            term = gate * _dot(y_ref[rows, :], w_ref[...])
            merged = term if merged is None else merged + term
        mix = _dot(merged.astype(BF16), wo_ref[...])
        o_ref[rows, :] = x + _rms_normalize(mix, gpost_ref[...])


def _merge(x2d, g_pre, y_swa, y_sb, y_mem, w_gl, w_swa, w_sb, w_mem, w_out, g_post):
    t = x2d.shape[0]
    row = lambda i: (i, 0)
    const = lambda i: (0, 0)
    resident = functools.partial(pl.BlockSpec, index_map=const, pipeline_mode=pl.Buffered(1))
    return pl.pallas_call(
        _merge_kernel,
        grid=(t // TM_MERGE,),
        in_specs=[
            pl.BlockSpec((TM_MERGE, D_MODEL), row),
            pl.BlockSpec((1, D_MODEL), const),
            pl.BlockSpec((TM_MERGE, SWA_Q_W), row),
            pl.BlockSpec((TM_MERGE, SB_W), row),
            pl.BlockSpec((TM_MERGE, MEM_W), row),
            resident((D_MODEL, N_BRANCH * D_MODEL)),
            resident((SWA_Q_W, D_MODEL)),
            resident((SB_W, D_MODEL)),
            resident((MEM_W, D_MODEL)),
            resident((D_MODEL, D_MODEL)),
            pl.BlockSpec((1, D_MODEL), const),
        ],
        out_specs=pl.BlockSpec((TM_MERGE, D_MODEL), row),
        out_shape=jax.ShapeDtypeStruct((t, D_MODEL), F32),
        compiler_params=pltpu.CompilerParams(
            dimension_semantics=("arbitrary",), vmem_limit_bytes=VMEM_LIMIT_BYTES),
        name="merge",
    )(x2d, g_pre, y_swa, y_sb, y_mem, w_gl, w_swa, w_sb, w_mem, w_out, g_post)


def _ffn_kernel(h_ref, gpre_ref, wg_ref, wu_ref, wd_ref, gpost_ref, o_ref):
    for rows in _row_groups(TM_FFN):
        h = h_ref[rows, :]
        u = _gained_bf16(h, gpre_ref[...])
        inv = _inv_rms(h)
        acc = None
        for start, width in FF_CHUNKS:
            gate = _dot(u, wg_ref[:, start:start + width]) * inv
            up = _dot(u, wu_ref[:, start:start + width]) * inv
            act = (gate * jax.nn.sigmoid(gate) * up).astype(BF16)
            part = _dot(act, wd_ref[start:start + width, :])
            acc = part if acc is None else acc + part
        o_ref[rows, :] = h + _rms_normalize(acc, gpost_ref[...])


def _ffn(h2d, g_pre, w_gate, w_up, w_down, g_post):
    t = h2d.shape[0]
    row = lambda i: (i, 0)
    const = lambda i: (0, 0)
    resident = functools.partial(pl.BlockSpec, index_map=const, pipeline_mode=pl.Buffered(1))
    return pl.pallas_call(
        _ffn_kernel,
        grid=(t // TM_FFN,),
        in_specs=[
            pl.BlockSpec((TM_FFN, D_MODEL), row),
            pl.BlockSpec((1, D_MODEL), const),
            resident((D_MODEL, D_FF)),
            resident((D_MODEL, D_FF)),
            resident((D_FF, D_MODEL)),
            pl.BlockSpec((1, D_MODEL), const),
        ],
        out_specs=pl.BlockSpec((TM_FFN, D_MODEL), row),
        out_shape=jax.ShapeDtypeStruct((t, D_MODEL), F32),
        compiler_params=pltpu.CompilerParams(
            dimension_semantics=("arbitrary",), vmem_limit_bytes=VMEM_LIMIT_BYTES),
        name="ffn",
    )(h2d, g_pre, w_gate, w_up, w_down, g_post)


def _prepare_w_in(w_in):
    scale = HEAD_DIM ** -0.5
    assert scale == 0.125
    offs = np.cumsum([0] + [w for _, w in PROJ_PIECES])
    qa = w_in[:, offs[0]:offs[1]].reshape(D_MODEL, SWA_Q_HEADS, HEAD_DIM)
    qa = (qa[:, np.array(SWA_HEAD_ORDER), :] * scale).reshape(D_MODEL, SWA_Q_W)
    qb = w_in[:, offs[3]:offs[4]] * scale
    w_qkv = jnp.concatenate([qa, w_in[:, offs[1]:offs[3]], qb, w_in[:, offs[4]:QKV_W]], axis=1)
    return w_qkv.astype(BF16), w_in[:, QKV_W:].astype(BF16)


def _prepare_w_branch_swa(w):
    w = w.reshape(SWA_Q_HEADS, HEAD_DIM, D_MODEL)[np.array(SWA_HEAD_ORDER)]
    return w.reshape(SWA_Q_W, D_MODEL).astype(BF16)


def kernel(x, mem, ln_mix_pre, ln_mix_post, w_in, swa_sinks, rel_bias, ln_mem, w_mem_kv,
           w_branch_swa, w_branch_sb, w_branch_mem, w_out, ln_ffn_pre, ln_ffn_post,
           w_gate, w_up, w_down):
    batch, seq, d = x.shape
    depth = w_in.shape[0]
    h = x.reshape(batch * seq, d)
    bias = _swa_bias_table(rel_bias)
    for l in range(depth):
        w_qkv, w_gl = _prepare_w_in(w_in[l])
        qa, ka, va, qb, kb, vb, qm = _in_proj(h, ln_mix_pre[l][None], w_qkv)
        y_swa = _swa_attention(qa, ka, va, swa_sinks[l].astype(F32), bias, batch, seq)
        y_sb = _sb_attention(qb, kb, vb, batch, seq)
        mk, mv = _mem_kv(mem, ln_mem[l][None], w_mem_kv[l].astype(BF16))
        y_mem = _mem_attention(qm, mk, mv, batch, seq)
        h = _merge(h, ln_mix_pre[l][None], y_swa.reshape(batch * seq, SWA_Q_W),
                   y_sb.reshape(batch * seq, SB_W), y_mem.reshape(batch * seq, MEM_W), w_gl,
                   _prepare_w_branch_swa(w_branch_swa[l]), w_branch_sb[l].astype(BF16),
                   w_branch_mem[l].astype(BF16), w_out[l].astype(BF16), ln_mix_post[l][None])
        h = _ffn(h, ln_ffn_pre[l][None], w_gate[l].astype(BF16), w_up[l].astype(BF16),
                 w_down[l].astype(BF16), ln_ffn_post[l][None])
    return h.reshape(batch, seq, d)
```

```python
import functools
import math

import numpy as np
import jax
import jax.numpy as jnp
from jax import lax
from jax.experimental import pallas as pl
from jax.experimental.pallas import tpu as pltpu

F32 = jnp.float32
BF16 = jnp.bfloat16

D_MODEL = 1024
BLOCK = 128
EPS = 1e-6
HEAD_DIM = 64
SWA_Q_HEADS = 8
SWA_KV_HEADS = 2
SWA_WINDOW = 128
N_BUCKETS = 32
MAX_DISTANCE = 128
SB_HEADS = 8
MEM_HEADS = 4
MEM_HEAD_DIM = 128
SWA_Q_W = SWA_Q_HEADS * HEAD_DIM
SWA_KV_W = SWA_KV_HEADS * HEAD_DIM
SB_W = SB_HEADS * HEAD_DIM
MEM_W = MEM_HEADS * MEM_HEAD_DIM
N_BRANCH = 3
D_FF = 2816

LANES = 128
VMEM_LIMIT_BYTES = 56 * 1024 * 1024

TM_PROJ = 1024
TM_MERGE = 1024
TM_FFN = 1024
TQ_SB = 256
KB_SB = 256
TQ_MEM = 512
SWA_BLOCKS_PER_STEP = 8
COL_CHUNK = 512
ROW_GROUP = 512
SB_DEAD_CARRY = 96.0
SB_TOP_ROWS = 160
FF_CHUNKS = ((0, 1024), (1024, 1024), (2048, 768))

PROJ_PIECES = (("qa", SWA_Q_W), ("ka", SWA_KV_W), ("va", SWA_KV_W), ("qb", SB_W),
               ("kb", SB_W), ("vb", SB_W), ("qm", MEM_W))
QKV_W = sum(w for _, w in PROJ_PIECES)

SWA_HEAD_ORDER = (0, 4, 1, 5, 2, 6, 3, 7)


def _inv_rms(x):
    return lax.rsqrt(jnp.mean(x * x, axis=-1, keepdims=True) + EPS)


def _rms_normalize(x, gain):
    return x * _inv_rms(x) * gain


def _gained_bf16(x, gain):
    return (x * gain).astype(BF16)


def _half_mask(shape, half):
    lane = lax.broadcasted_iota(jnp.int32, shape, len(shape) - 1)
    return (lane < HEAD_DIM) if half == 0 else (lane >= HEAD_DIM)


def _row_groups(rows):
    return [slice(s, s + ROW_GROUP) for s in range(0, rows, ROW_GROUP)]


def _dot_nt(a, b):
    return lax.dot_general(a, b, (((1,), (1,)), ((), ())), preferred_element_type=F32)


def _dot(a, b):
    return jnp.dot(a, b, preferred_element_type=F32)


def _in_proj_kernel(x_ref, g_ref, w_ref, *out_refs):
    x = x_ref[...]
    u = _gained_bf16(x, g_ref[...])
    inv = _inv_rms(x)
    starts = np.cumsum([0] + [w for _, w in PROJ_PIECES])
    for c0 in range(0, QKV_W, COL_CHUNK):
        c1 = min(c0 + COL_CHUNK, QKV_W)
        y = (_dot(u, w_ref[:, c0:c1]) * inv).astype(BF16)
        for out_ref, p0, p1 in zip(out_refs, starts[:-1], starts[1:]):
            lo, hi = max(c0, p0), min(c1, p1)
            if lo < hi:
                out_ref[:, lo - p0:hi - p0] = y[:, lo - c0:hi - c0]


def _in_proj(x2d, gain, w_bf16):
    t = x2d.shape[0]
    in_w = w_bf16.shape[1]
    out_shape = [jax.ShapeDtypeStruct((t, w), BF16) for _, w in PROJ_PIECES]
    out_specs = [pl.BlockSpec((TM_PROJ, w), lambda i: (i, 0)) for _, w in PROJ_PIECES]
    return pl.pallas_call(
        _in_proj_kernel,
        grid=(t // TM_PROJ,),
        in_specs=[
            pl.BlockSpec((TM_PROJ, D_MODEL), lambda i: (i, 0)),
            pl.BlockSpec((1, D_MODEL), lambda i: (0, 0)),
            pl.BlockSpec((D_MODEL, in_w), lambda i: (0, 0), pipeline_mode=pl.Buffered(1)),
        ],
        out_specs=out_specs,
        out_shape=out_shape,
        compiler_params=pltpu.CompilerParams(
            dimension_semantics=("arbitrary",), vmem_limit_bytes=VMEM_LIMIT_BYTES),
        name="in_proj",
    )(x2d, gain, w_bf16)


def _swa_kernel(sink_ref, q_ref, kp_ref, kc_ref, vp_ref, vc_ref, bias_ref, o_ref, qs_ref):
    i = pl.program_id(1)
    half0 = _half_mask((BLOCK, LANES), 0)
    key = lax.broadcasted_iota(jnp.int32, (BLOCK, BLOCK), 0)
    qry = lax.broadcasted_iota(jnp.int32, (BLOCK, BLOCK), 1)
    from_prev = key > qry
    missing = jnp.logical_and(from_prev, i == 0)
    first_kv = key < HEAD_DIM
    rows = SWA_Q_HEADS * BLOCK

    scores, vbands = [], []
    for j in range(SWA_BLOCKS_PER_STEP):
        blk = slice(j * BLOCK, (j + 1) * BLOCK)
        for p in range(SWA_Q_HEADS // 2):
            q2 = q_ref[blk, p * LANES:(p + 1) * LANES]
            zero = jnp.zeros_like(q2)
            base = j * rows + 2 * p * BLOCK
            qs_ref[base:base + BLOCK, :] = jnp.where(half0, q2, zero)
            qs_ref[base + BLOCK:base + 2 * BLOCK, :] = jnp.where(half0, zero, q2)
        if j == 0:
            kband = jnp.concatenate([kp_ref[...], kc_ref[blk, :]], axis=0)
            vband = jnp.concatenate([vp_ref[...], vc_ref[blk, :]], axis=0)
        else:
            band = slice((j - 1) * BLOCK, (j + 1) * BLOCK)
            kband, vband = kc_ref[band, :], vc_ref[band, :]
        scores.append(_dot_nt(kband, qs_ref[j * rows:(j + 1) * rows, :]))
        vbands.append(vband)

    for j in range(SWA_BLOCKS_PER_STEP):
        weights, recips = [], []
        for h in range(SWA_Q_HEADS):
            head = SWA_HEAD_ORDER[h]
            zh = scores[j][:, h * BLOCK:(h + 1) * BLOCK]
            s = jnp.where(from_prev, zh[:BLOCK], zh[BLOCK:]) + bias_ref[head]
            if j == 0:
                s = jnp.where(missing, -jnp.inf, s)
            sink = sink_ref[head]
            m = jnp.maximum(jnp.max(s, axis=0, keepdims=True), sink)
            pr = jnp.exp(s - m)
            denom = jnp.sum(pr, axis=0, keepdims=True) + jnp.exp(sink - m)
            recips.append(1.0 / denom)
            zero = jnp.zeros_like(pr)
            unfolded = jnp.concatenate(
                [jnp.where(from_prev, pr, zero), jnp.where(from_prev, zero, pr)], axis=0)
            weights.append(unfolded.astype(BF16))
        v_t = vbands[j].astype(F32).T.astype(BF16)
        o_t = _dot(v_t, jnp.concatenate(weights, axis=1))
        for p in range(SWA_Q_HEADS // 2):
            t0 = o_t[:, (2 * p) * BLOCK:(2 * p + 1) * BLOCK] * recips[2 * p]
            t1 = o_t[:, (2 * p + 1) * BLOCK:(2 * p + 2) * BLOCK] * recips[2 * p + 1]
            tile = jnp.where(first_kv, t0, t1).T
            o_ref[j * BLOCK:(j + 1) * BLOCK, p * LANES:(p + 1) * LANES] = tile.astype(BF16)


def _swa_attention(qa, ka, va, sinks, bias, batch, seq):
    tq = SWA_BLOCKS_PER_STEP * BLOCK
    qa = qa.reshape(batch, seq, SWA_Q_W)
    ka = ka.reshape(batch, seq, SWA_KV_W)
    va = va.reshape(batch, seq, SWA_KV_W)
    cur = lambda b, i: (b, i, 0)
    prev = lambda b, i: (b, jnp.maximum(i * SWA_BLOCKS_PER_STEP - 1, 0), 0)
    return pl.pallas_call(
        _swa_kernel,
        grid=(batch, seq // tq),
        in_specs=[
            pl.BlockSpec(memory_space=pltpu.SMEM),
            pl.BlockSpec((None, tq, SWA_Q_W), cur),
            pl.BlockSpec((None, BLOCK, SWA_KV_W), prev),
            pl.BlockSpec((None, tq, SWA_KV_W), cur),
            pl.BlockSpec((None, BLOCK, SWA_KV_W), prev),
            pl.BlockSpec((None, tq, SWA_KV_W), cur),
            pl.BlockSpec((SWA_Q_HEADS, BLOCK, BLOCK), lambda b, i: (0, 0, 0)),
        ],
        out_specs=pl.BlockSpec((None, tq, SWA_Q_W), cur),
        out_shape=jax.ShapeDtypeStruct((batch, seq, SWA_Q_W), BF16),
        scratch_shapes=[pltpu.VMEM((SWA_BLOCKS_PER_STEP * SWA_Q_HEADS * BLOCK, LANES), BF16)],
        compiler_params=pltpu.CompilerParams(
            dimension_semantics=("arbitrary", "arbitrary"), vmem_limit_bytes=VMEM_LIMIT_BYTES),
        name="swa_attention",
    )(sinks, qa, ka, ka, va, va, bias)


def _t5_bucket(dist):
    max_exact = N_BUCKETS // 2
    d = jnp.maximum(dist, 0)
    df = jnp.maximum(d, 1).astype(F32)
    large = max_exact + (jnp.log(df / max_exact) / math.log(MAX_DISTANCE / max_exact)
                         * (N_BUCKETS - max_exact)).astype(jnp.int32)
    large = jnp.minimum(large, N_BUCKETS - 1)
    return jnp.where(d < max_exact, d, large)


def _swa_bias_table(rel_bias):
    c = jnp.arange(BLOCK)[:, None]
    r = jnp.arange(BLOCK)[None, :]
    dist = jnp.where(c > r, r + BLOCK - c, r - c)
    assert SWA_WINDOW == BLOCK
    bucket = _t5_bucket(dist)
    rb = rel_bias.astype(F32)
    bias = jnp.zeros((SWA_Q_HEADS, BLOCK, BLOCK), F32)
    for b in range(N_BUCKETS):
        bias = jnp.where((bucket == b)[None], rb[b][:, None, None], bias)
    return bias


def _neg_abs(x):
    bits = lax.bitcast_convert_type(x, jnp.int32) | jnp.int32(-2 ** 31)
    return lax.bitcast_convert_type(bits, F32)


def _sb_kernel(q_ref, k_ref, v_ref, o_ref, qs_ref, acc_ref, carry_ref):
    i = pl.program_id(1)
    n_pairs = SB_HEADS // 2
    top, rest = SB_TOP_ROWS, TQ_SB - SB_TOP_ROWS
    rows = 2 * TQ_SB
    half0_top = _half_mask((top, LANES), 0)
    half0_rest = _half_mask((rest, LANES), 0)
    r = lax.broadcasted_iota(jnp.int32, (KB_SB, KB_SB), 0)
    c = lax.broadcasted_iota(jnp.int32, (KB_SB, KB_SB), 1)
    suffix = jnp.where(r > c, 1.0, 0.0).astype(BF16)
    rr = lax.broadcasted_iota(jnp.int32, (rows, KB_SB), 0)
    cc = lax.broadcasted_iota(jnp.int32, (rows, KB_SB), 1)
    tile_row = jnp.where(rr < top, rr, jnp.where(rr < 2 * top + rest, rr - top, rr - top - rest))
    causal = cc < tile_row

    for p in range(n_pairs):
        q2 = q_ref[:, p * LANES:(p + 1) * LANES]
        q_top, q_rest = q2[:top], q2[top:]
        qs_ref[p, 0:top, :] = jnp.where(half0_top, q_top, jnp.zeros_like(q_top))
        qs_ref[p, top:2 * top, :] = jnp.where(half0_top, jnp.zeros_like(q_top), q_top)
        qs_ref[p, 2 * top:2 * top + rest, :] = jnp.where(half0_rest, q_rest, jnp.zeros_like(q_rest))
        qs_ref[p, 2 * top + rest:, :] = jnp.where(half0_rest, jnp.zeros_like(q_rest), q_rest)
    lanes = [slice(p * LANES, (p + 1) * LANES) for p in range(n_pairs)]

    def visit(j, diagonal, top_only):
        n = 2 * top if top_only else rows
        start = pl.multiple_of(j * KB_SB, KB_SB)
        zs = [_dot_nt(qs_ref[p, 0:n, :], k_ref[pl.ds(start, KB_SB), lanes[p]])
              for p in range(n_pairs)]
        sps, lzs, firsts = [], [], []
        for z in zs:
            sp = jnp.maximum(z, 0.0) + jnp.log(1.0 + jnp.exp(_neg_abs(z)))
            lz = z - sp
            if diagonal:
                sp = jnp.where(causal, sp, 0.0)
                lz = jnp.where(causal, lz, -jnp.inf)
            sps.append(sp.astype(BF16))
            lzs.append(lz)
            firsts.append(sp[:, 0:1])
        within_all = _dot(jnp.concatenate(sps, axis=0), suffix)
        least = None
        for p in range(n_pairs):
            within = within_all[p * n:(p + 1) * n]
            total = within[:, 0:1] + firsts[p]
            if diagonal:
                a = jnp.exp(lzs[p] - within)
            else:
                carry = carry_ref[p, 0:n, :]
                a = jnp.exp(lzs[p] - (within + carry))
                total = carry + total
            least = total if least is None else jnp.minimum(least, total)
            pv = _dot(a.astype(BF16), v_ref[pl.ds(start, KB_SB), lanes[p]])
            upd_top = jnp.where(half0_top, pv[0:top], pv[top:2 * top])
            if top_only:
                acc_ref[p, 0:top, :] += upd_top
            else:
                upd_rest = jnp.where(half0_rest, pv[2 * top:2 * top + rest], pv[2 * top + rest:])
                upd = jnp.concatenate([upd_top, upd_rest], axis=0)
                if diagonal:
                    acc_ref[p] = upd
                else:
                    acc_ref[p] += upd
            carry_ref[p, 0:n, :] = total
        least_top = jnp.min(least[0:2 * top])
        least_rest = jnp.float32(jnp.inf) if top_only else jnp.min(least[2 * top:])
        return least_top, least_rest

    least_top, least_rest = visit(i, True, False)

    def rest_live(state):
        n, _, least_rest = state
        return jnp.logical_and(n < i, least_rest < SB_DEAD_CARRY)

    def full_visit(state):
        n = state[0]
        return (n + 1,) + visit(i - 1 - n, False, False)

    def top_live(state):
        n, least_top = state
        return jnp.logical_and(n < i, least_top < SB_DEAD_CARRY)

    def top_visit(state):
        n = state[0]
        return n + 1, visit(i - 1 - n, False, True)[0]

    n, least_top, _ = lax.while_loop(rest_live, full_visit, (jnp.int32(0), least_top, least_rest))
    lax.while_loop(top_live, top_visit, (n, least_top))
    for p in range(n_pairs):
        o_ref[:, p * LANES:(p + 1) * LANES] = acc_ref[p].astype(BF16)


def _sb_attention(qb, kb, vb, batch, seq):
    qb = qb.reshape(batch, seq, SB_W)
    kb = kb.reshape(batch, seq, SB_W)
    vb = vb.reshape(batch, seq, SB_W)
    assert TQ_SB == KB_SB
    q_map = lambda b, i: (b, i, 0)
    kv_map = lambda b, i: (b, 0, 0)
    return pl.pallas_call(
        _sb_kernel,
        grid=(batch, seq // TQ_SB),
        in_specs=[
            pl.BlockSpec((None, TQ_SB, SB_W), q_map),
            pl.BlockSpec((None, seq, SB_W), kv_map),
            pl.BlockSpec((None, seq, SB_W), kv_map),
        ],
        out_specs=pl.BlockSpec((None, TQ_SB, SB_W), q_map),
        out_shape=jax.ShapeDtypeStruct((batch, seq, SB_W), BF16),
        scratch_shapes=[
            pltpu.VMEM((SB_HEADS // 2, 2 * TQ_SB, LANES), BF16),
            pltpu.VMEM((SB_HEADS // 2, TQ_SB, LANES), F32),
            pltpu.VMEM((SB_HEADS // 2, 2 * TQ_SB, 1), F32),
        ],
        compiler_params=pltpu.CompilerParams(
            dimension_semantics=("arbitrary", "arbitrary"),
            vmem_limit_bytes=VMEM_LIMIT_BYTES),
        name="sb_attention",
    )(qb, kb, vb)


def _mem_kv_kernel(m_ref, g_ref, w_ref, k_ref, v_ref):
    u = _rms_normalize(m_ref[...], g_ref[...]).astype(BF16)
    k_ref[...] = _dot(u, w_ref[:, :MEM_W]).astype(BF16)
    v_ref[...] = _dot(u, w_ref[:, MEM_W:]).astype(BF16)


def _mem_kv(mem, gain, w_bf16):
    batch, mem_len, _ = mem.shape
    blk = lambda b: (b, 0, 0)
    return pl.pallas_call(
        _mem_kv_kernel,
        grid=(batch,),
        in_specs=[
            pl.BlockSpec((None, mem_len, D_MODEL), blk),
            pl.BlockSpec((1, D_MODEL), lambda b: (0, 0)),
            pl.BlockSpec((D_MODEL, 2 * MEM_W), lambda b: (0, 0)),
        ],
        out_specs=[pl.BlockSpec((None, mem_len, MEM_W), blk)] * 2,
        out_shape=[jax.ShapeDtypeStruct((batch, mem_len, MEM_W), BF16)] * 2,
        compiler_params=pltpu.CompilerParams(
            dimension_semantics=("arbitrary",), vmem_limit_bytes=VMEM_LIMIT_BYTES),
        name="mem_kv",
    )(mem, gain, w_bf16)


def _mem_attn_kernel(q_ref, k_ref, v_ref, o_ref):
    c = (MEM_HEAD_DIM ** -0.5) * math.log2(math.e)
    heads = [slice(h * MEM_HEAD_DIM, (h + 1) * MEM_HEAD_DIM) for h in range(MEM_HEADS)]
    zs = [_dot_nt(q_ref[:, sl], k_ref[:, sl]) for sl in heads]
    for sl, z in zip(heads, zs):
        pr = jnp.exp2((z - jnp.max(z, axis=-1, keepdims=True)) * c)
        recip = 1.0 / jnp.sum(pr, axis=-1, keepdims=True)
        o_ref[:, sl] = (_dot(pr.astype(BF16), v_ref[:, sl]) * recip).astype(BF16)


def _mem_attention(qm, mk, mv, batch, seq):
    qm = qm.reshape(batch, seq, MEM_W)
    mem_len = mk.shape[1]
    q_map = lambda b, i: (b, i, 0)
    kv_map = lambda b, i: (b, 0, 0)
    return pl.pallas_call(
        _mem_attn_kernel,
        grid=(batch, seq // TQ_MEM),
        in_specs=[
            pl.BlockSpec((None, TQ_MEM, MEM_W), q_map),
            pl.BlockSpec((None, mem_len, MEM_W), kv_map),
            pl.BlockSpec((None, mem_len, MEM_W), kv_map),
        ],
        out_specs=pl.BlockSpec((None, TQ_MEM, MEM_W), q_map),
        out_shape=jax.ShapeDtypeStruct((batch, seq, MEM_W), BF16),
        compiler_params=pltpu.CompilerParams(
            dimension_semantics=("arbitrary", "arbitrary"), vmem_limit_bytes=VMEM_LIMIT_BYTES),
        name="mem_attention",
    )(qm, mk, mv)


def _merge_kernel(x_ref, gpre_ref, ya_ref, yb_ref, ym_ref, wgl_ref, wa_ref, wb_ref, wm_ref,
                  wo_ref, gpost_ref, o_ref):
    for rows in _row_groups(TM_MERGE):
        x = x_ref[rows, :]
        u = _gained_bf16(x, gpre_ref[...])
        inv = _inv_rms(x)
        merged = None
        for n, (y_ref, w_ref) in enumerate(((ya_ref, wa_ref), (yb_ref, wb_ref), (ym_ref, wm_ref))):
            gate = jax.nn.sigmoid(_dot(u, wgl_ref[:, n * D_MODEL:(n + 1) * D_MODEL]) * inv)
            term = gate * _dot(y_ref[rows, :], w_ref[...])
            merged = term if merged is None else merged + term
        mix = _dot(merged.astype(BF16), wo_ref[...])
        o_ref[rows, :] = x + _rms_normalize(mix, gpost_ref[...])


def _merge(x2d, g_pre, y_swa, y_sb, y_mem, w_gl, w_swa, w_sb, w_mem, w_out, g_post):
    t = x2d.shape[0]
    row = lambda i: (i, 0)
    const = lambda i: (0, 0)
    resident = functools.partial(pl.BlockSpec, index_map=const, pipeline_mode=pl.Buffered(1))
    return pl.pallas_call(
        _merge_kernel,
        grid=(t // TM_MERGE,),
        in_specs=[
            pl.BlockSpec((TM_MERGE, D_MODEL), row),
            pl.BlockSpec((1, D_MODEL), const),
            pl.BlockSpec((TM_MERGE, SWA_Q_W), row),
            pl.BlockSpec((TM_MERGE, SB_W), row),
            pl.BlockSpec((TM_MERGE, MEM_W), row),
            resident((D_MODEL, N_BRANCH * D_MODEL)),
            resident((SWA_Q_W, D_MODEL)),
            resident((SB_W, D_MODEL)),
            resident((MEM_W, D_MODEL)),
            resident((D_MODEL, D_MODEL)),
            pl.BlockSpec((1, D_MODEL), const),
        ],
        out_specs=pl.BlockSpec((TM_MERGE, D_MODEL), row),
        out_shape=jax.ShapeDtypeStruct((t, D_MODEL), F32),
        compiler_params=pltpu.CompilerParams(
            dimension_semantics=("arbitrary",), vmem_limit_bytes=VMEM_LIMIT_BYTES),
        name="merge",
    )(x2d, g_pre, y_swa, y_sb, y_mem, w_gl, w_swa, w_sb, w_mem, w_out, g_post)


def _ffn_kernel(h_ref, gpre_ref, wg_ref, wu_ref, wd_ref, gpost_ref, o_ref):
    for rows in _row_groups(TM_FFN):
        h = h_ref[rows, :]
        u = _gained_bf16(h, gpre_ref[...])
        inv = _inv_rms(h)
        acc = None
        for start, width in FF_CHUNKS:
            gate = _dot(u, wg_ref[:, start:start + width]) * inv
            up = _dot(u, wu_ref[:, start:start + width]) * inv
            act = (gate * jax.nn.sigmoid(gate) * up).astype(BF16)
            part = _dot(act, wd_ref[start:start + width, :])
            acc = part if acc is None else acc + part
        o_ref[rows, :] = h + _rms_normalize(acc, gpost_ref[...])


def _ffn(h2d, g_pre, w_gate, w_up, w_down, g_post):
    t = h2d.shape[0]
    row = lambda i: (i, 0)
    const = lambda i: (0, 0)
    resident = functools.partial(pl.BlockSpec, index_map=const, pipeline_mode=pl.Buffered(1))
    return pl.pallas_call(
        _ffn_kernel,
        grid=(t // TM_FFN,),
        in_specs=[
            pl.BlockSpec((TM_FFN, D_MODEL), row),
            pl.BlockSpec((1, D_MODEL), const),
            resident((D_MODEL, D_FF)),
            resident((D_MODEL, D_FF)),
            resident((D_FF, D_MODEL)),
            pl.BlockSpec((1, D_MODEL), const),
        ],
        out_specs=pl.BlockSpec((TM_FFN, D_MODEL), row),
        out_shape=jax.ShapeDtypeStruct((t, D_MODEL), F32),
        compiler_params=pltpu.CompilerParams(
            dimension_semantics=("arbitrary",), vmem_limit_bytes=VMEM_LIMIT_BYTES),
        name="ffn",
    )(h2d, g_pre, w_gate, w_up, w_down, g_post)


def _prepare_w_in(w_in):
    scale = HEAD_DIM ** -0.5
    assert scale == 0.125
    offs = np.cumsum([0] + [w for _, w in PROJ_PIECES])
    qa = w_in[:, offs[0]:offs[1]].reshape(D_MODEL, SWA_Q_HEADS, HEAD_DIM)
    qa = (qa[:, np.array(SWA_HEAD_ORDER), :] * scale).reshape(D_MODEL, SWA_Q_W)
    qb = w_in[:, offs[3]:offs[4]] * scale
    w_qkv = jnp.concatenate([qa, w_in[:, offs[1]:offs[3]], qb, w_in[:, offs[4]:QKV_W]], axis=1)
    return w_qkv.astype(BF16), w_in[:, QKV_W:].astype(BF16)


def _prepare_w_branch_swa(w):
    w = w.reshape(SWA_Q_HEADS, HEAD_DIM, D_MODEL)[np.array(SWA_HEAD_ORDER)]
    return w.reshape(SWA_Q_W, D_MODEL).astype(BF16)


def kernel(x, mem, ln_mix_pre, ln_mix_post, w_in, swa_sinks, rel_bias, ln_mem, w_mem_kv,
           w_branch_swa, w_branch_sb, w_branch_mem, w_out, ln_ffn_pre, ln_ffn_post,
           w_gate, w_up, w_down):
    batch, seq, d = x.shape
    depth = w_in.shape[0]
    h = x.reshape(batch * seq, d)
    bias = _swa_bias_table(rel_bias)
    for l in range(depth):
        w_qkv, w_gl = _prepare_w_in(w_in[l])
        qa, ka, va, qb, kb, vb, qm = _in_proj(h, ln_mix_pre[l][None], w_qkv)
        y_swa = _swa_attention(qa, ka, va, swa_sinks[l].astype(F32), bias, batch, seq)
        y_sb = _sb_attention(qb, kb, vb, batch, seq)
        mk, mv = _mem_kv(mem, ln_mem[l][None], w_mem_kv[l].astype(BF16))
        y_mem = _mem_attention(qm, mk, mv, batch, seq)
        h = _merge(h, ln_mix_pre[l][None], y_swa.reshape(batch * seq, SWA_Q_W),
                   y_sb.reshape(batch * seq, SB_W), y_mem.reshape(batch * seq, MEM_W), w_gl,
                   _prepare_w_branch_swa(w_branch_swa[l]), w_branch_sb[l].astype(BF16),
                   w_branch_mem[l].astype(BF16), w_out[l].astype(BF16), ln_mix_post[l][None])
        h = _ffn(h, ln_ffn_pre[l][None], w_gate[l].astype(BF16), w_up[l].astype(BF16),
                 w_down[l].astype(BF16), ln_ffn_post[l][None])
    return h.reshape(batch, seq, d)
```

```python
import functools
import math

import numpy as np
import jax
import jax.numpy as jnp
from jax import lax
from jax.experimental import pallas as pl
from jax.experimental.pallas import tpu as pltpu

F32 = jnp.float32
BF16 = jnp.bfloat16

D_MODEL = 1024
BLOCK = 128
EPS = 1e-6
HEAD_DIM = 64
SWA_Q_HEADS = 8
SWA_KV_HEADS = 2
SWA_WINDOW = 128
N_BUCKETS = 32
MAX_DISTANCE = 128
SB_HEADS = 8
MEM_HEADS = 4
MEM_HEAD_DIM = 128
SWA_Q_W = SWA_Q_HEADS * HEAD_DIM
SWA_KV_W = SWA_KV_HEADS * HEAD_DIM
SB_W = SB_HEADS * HEAD_DIM
MEM_W = MEM_HEADS * MEM_HEAD_DIM
N_BRANCH = 3
D_FF = 2816

LANES = 128
VMEM_LIMIT_BYTES = 56 * 1024 * 1024

TM_PROJ = 1024
TM_MERGE = 1024
TM_FFN = 1024
TQ_SB = 256
KB_SB = 256
COL_CHUNK = 512
ROW_GROUP = 512
SB_DEAD_CARRY = 96.0
SB_TOP_ROWS = 160
FF_CHUNKS = ((0, 1024), (1024, 1024), (2048, 768))

PROJ_PIECES = (("qa", SWA_Q_W), ("ka", SWA_KV_W), ("va", SWA_KV_W), ("qb", SB_W),
               ("kb", SB_W), ("vb", SB_W), ("qm", MEM_W))
QKV_W = sum(w for _, w in PROJ_PIECES)

SWA_HEAD_ORDER = (0, 4, 1, 5, 2, 6, 3, 7)


def _inv_rms(x):
    return lax.rsqrt(jnp.mean(x * x, axis=-1, keepdims=True) + EPS)


def _rms_normalize(x, gain):
    return x * _inv_rms(x) * gain


def _gained_bf16(x, gain):
    return (x * gain).astype(BF16)


def _half_mask(shape, half):
    lane = lax.broadcasted_iota(jnp.int32, shape, len(shape) - 1)
    return (lane < HEAD_DIM) if half == 0 else (lane >= HEAD_DIM)


def _row_groups(rows):
    return [slice(s, s + ROW_GROUP) for s in range(0, rows, ROW_GROUP)]


def _dot_nt(a, b):
    return lax.dot_general(a, b, (((1,), (1,)), ((), ())), preferred_element_type=F32)


def _dot(a, b):
    return jnp.dot(a, b, preferred_element_type=F32)


def _in_proj_kernel(x_ref, g_ref, w_ref, *out_refs):
    x = x_ref[...]
    u = _gained_bf16(x, g_ref[...])
    inv = _inv_rms(x)
    starts = np.cumsum([0] + [w for _, w in PROJ_PIECES])
    for c0 in range(0, QKV_W, COL_CHUNK):
        c1 = min(c0 + COL_CHUNK, QKV_W)
        y = (_dot(u, w_ref[:, c0:c1]) * inv).astype(BF16)
        for out_ref, p0, p1 in zip(out_refs, starts[:-1], starts[1:]):
            lo, hi = max(c0, p0), min(c1, p1)
            if lo < hi:
                out_ref[:, lo - p0:hi - p0] = y[:, lo - c0:hi - c0]


def _in_proj(x2d, gain, w_bf16):
    t = x2d.shape[0]
    in_w = w_bf16.shape[1]
    out_shape = [jax.ShapeDtypeStruct((t, w), BF16) for _, w in PROJ_PIECES]
    out_specs = [pl.BlockSpec((TM_PROJ, w), lambda i: (i, 0)) for _, w in PROJ_PIECES]
    return pl.pallas_call(
        _in_proj_kernel,
        grid=(t // TM_PROJ,),
        in_specs=[
            pl.BlockSpec((TM_PROJ, D_MODEL), lambda i: (i, 0)),
            pl.BlockSpec((1, D_MODEL), lambda i: (0, 0)),
            pl.BlockSpec((D_MODEL, in_w), lambda i: (0, 0), pipeline_mode=pl.Buffered(1)),
        ],
        out_specs=out_specs,
        out_shape=out_shape,
        compiler_params=pltpu.CompilerParams(
            dimension_semantics=("arbitrary",), vmem_limit_bytes=VMEM_LIMIT_BYTES),
        name="in_proj",
    )(x2d, gain, w_bf16)


def _swa_scores(j, q_ref, kp_ref, kc_ref, vp_ref, vc_ref, qs_ref):
    half0 = _half_mask((BLOCK, LANES), 0)
    rows = SWA_Q_HEADS * BLOCK
    blk = slice(j * BLOCK, (j + 1) * BLOCK)
    for p in range(SWA_Q_HEADS // 2):
        q2 = q_ref[blk, p * LANES:(p + 1) * LANES]
        zero = jnp.zeros_like(q2)
        base = j * rows + 2 * p * BLOCK
        qs_ref[base:base + BLOCK, :] = jnp.where(half0, q2, zero)
        qs_ref[base + BLOCK:base + 2 * BLOCK, :] = jnp.where(half0, zero, q2)
    if j == 0:
        kband = jnp.concatenate([kp_ref[...], kc_ref[blk, :]], axis=0)
        vband = jnp.concatenate([vp_ref[...], vc_ref[blk, :]], axis=0)
    else:
        band = slice((j - 1) * BLOCK, (j + 1) * BLOCK)
        kband, vband = kc_ref[band, :], vc_ref[band, :]
    return _dot_nt(kband, qs_ref[j * rows:(j + 1) * rows, :]), vband


def _swa_values(j, scores, vband, sink_ref, bias_ref, first_of_sequence, y_ref):
    key = lax.broadcasted_iota(jnp.int32, (BLOCK, BLOCK), 0)
    qry = lax.broadcasted_iota(jnp.int32, (BLOCK, BLOCK), 1)
    from_prev = key > qry
    first_kv = key < HEAD_DIM
    weights, recips = [], []
    for h in range(SWA_Q_HEADS):
        head = SWA_HEAD_ORDER[h]
        zh = scores[:, h * BLOCK:(h + 1) * BLOCK]
        s = jnp.where(from_prev, zh[:BLOCK], zh[BLOCK:]) + bias_ref[head]
        if j == 0:
            s = jnp.where(jnp.logical_and(from_prev, first_of_sequence), -jnp.inf, s)
        sink = sink_ref[head]
        m = jnp.maximum(jnp.max(s, axis=0, keepdims=True), sink)
        pr = jnp.exp(s - m)
        denom = jnp.sum(pr, axis=0, keepdims=True) + jnp.exp(sink - m)
        recips.append(1.0 / denom)
        zero = jnp.zeros_like(pr)
        unfolded = jnp.concatenate(
            [jnp.where(from_prev, pr, zero), jnp.where(from_prev, zero, pr)], axis=0)
        weights.append(unfolded.astype(BF16))
    v_t = vband.astype(F32).T.astype(BF16)
    o_t = _dot(v_t, jnp.concatenate(weights, axis=1))
    for p in range(SWA_Q_HEADS // 2):
        t0 = o_t[:, (2 * p) * BLOCK:(2 * p + 1) * BLOCK] * recips[2 * p]
        t1 = o_t[:, (2 * p + 1) * BLOCK:(2 * p + 2) * BLOCK] * recips[2 * p + 1]
        tile = jnp.where(first_kv, t0, t1).T
        y_ref[j * BLOCK:(j + 1) * BLOCK, p * LANES:(p + 1) * LANES] = tile.astype(BF16)


def _t5_bucket(dist):
    max_exact = N_BUCKETS // 2
    d = jnp.maximum(dist, 0)
    df = jnp.maximum(d, 1).astype(F32)
    large = max_exact + (jnp.log(df / max_exact) / math.log(MAX_DISTANCE / max_exact)
                         * (N_BUCKETS - max_exact)).astype(jnp.int32)
    large = jnp.minimum(large, N_BUCKETS - 1)
    return jnp.where(d < max_exact, d, large)


def _swa_bias_table(rel_bias):
    c = jnp.arange(BLOCK)[:, None]
    r = jnp.arange(BLOCK)[None, :]
    dist = jnp.where(c > r, r + BLOCK - c, r - c)
    assert SWA_WINDOW == BLOCK
    bucket = _t5_bucket(dist)
    rb = rel_bias.astype(F32)
    bias = jnp.zeros((SWA_Q_HEADS, BLOCK, BLOCK), F32)
    for b in range(N_BUCKETS):
        bias = jnp.where((bucket == b)[None], rb[b][:, None, None], bias)
    return bias


def _neg_abs(x):
    bits = lax.bitcast_convert_type(x, jnp.int32) | jnp.int32(-2 ** 31)
    return lax.bitcast_convert_type(bits, F32)


def _sb_kernel(q_ref, k_ref, v_ref, o_ref, qs_ref, acc_ref, carry_ref):
    i = pl.program_id(1)
    n_pairs = SB_HEADS // 2
    top, rest = SB_TOP_ROWS, TQ_SB - SB_TOP_ROWS
    rows = 2 * TQ_SB
    half0_top = _half_mask((top, LANES), 0)
    half0_rest = _half_mask((rest, LANES), 0)
    r = lax.broadcasted_iota(jnp.int32, (KB_SB, KB_SB), 0)
    c = lax.broadcasted_iota(jnp.int32, (KB_SB, KB_SB), 1)
    suffix = jnp.where(r > c, 1.0, 0.0).astype(BF16)
    rr = lax.broadcasted_iota(jnp.int32, (rows, KB_SB), 0)
    cc = lax.broadcasted_iota(jnp.int32, (rows, KB_SB), 1)
    tile_row = jnp.where(rr < top, rr, jnp.where(rr < 2 * top + rest, rr - top, rr - top - rest))
    causal = cc < tile_row

    for p in range(n_pairs):
        q2 = q_ref[:, p * LANES:(p + 1) * LANES]
        q_top, q_rest = q2[:top], q2[top:]
        qs_ref[p, 0:top, :] = jnp.where(half0_top, q_top, jnp.zeros_like(q_top))
        qs_ref[p, top:2 * top, :] = jnp.where(half0_top, jnp.zeros_like(q_top), q_top)
        qs_ref[p, 2 * top:2 * top + rest, :] = jnp.where(half0_rest, q_rest, jnp.zeros_like(q_rest))
        qs_ref[p, 2 * top + rest:, :] = jnp.where(half0_rest, jnp.zeros_like(q_rest), q_rest)
    lanes = [slice(p * LANES, (p + 1) * LANES) for p in range(n_pairs)]

    def visit(j, diagonal, top_only):
        n = 2 * top if top_only else rows
        start = pl.multiple_of(j * KB_SB, KB_SB)
        zs = [_dot_nt(qs_ref[p, 0:n, :], k_ref[pl.ds(start, KB_SB), lanes[p]])
              for p in range(n_pairs)]
        sps, lzs, firsts = [], [], []
        for z in zs:
            sp = jnp.maximum(z, 0.0) + jnp.log(1.0 + jnp.exp(_neg_abs(z)))
            lz = z - sp
            if diagonal:
                sp = jnp.where(causal, sp, 0.0)
                lz = jnp.where(causal, lz, -jnp.inf)
            sps.append(sp.astype(BF16))
            lzs.append(lz)
            firsts.append(sp[:, 0:1])
        within_all = _dot(jnp.concatenate(sps, axis=0), suffix)
        least = None
        for p in range(n_pairs):
            within = within_all[p * n:(p + 1) * n]
            total = within[:, 0:1] + firsts[p]
            if diagonal:
                a = jnp.exp(lzs[p] - within)
            else:
                carry = carry_ref[p, 0:n, :]
                a = jnp.exp(lzs[p] - (within + carry))
                total = carry + total
            least = total if least is None else jnp.minimum(least, total)
            pv = _dot(a.astype(BF16), v_ref[pl.ds(start, KB_SB), lanes[p]])
            upd_top = jnp.where(half0_top, pv[0:top], pv[top:2 * top])
            if top_only:
                acc_ref[p, 0:top, :] += upd_top
            else:
                upd_rest = jnp.where(half0_rest, pv[2 * top:2 * top + rest], pv[2 * top + rest:])
                upd = jnp.concatenate([upd_top, upd_rest], axis=0)
                if diagonal:
                    acc_ref[p] = upd
                else:
                    acc_ref[p] += upd
            carry_ref[p, 0:n, :] = total
        least_top = jnp.min(least[0:2 * top])
        least_rest = jnp.float32(jnp.inf) if top_only else jnp.min(least[2 * top:])
        return least_top, least_rest

    least_top, least_rest = visit(i, True, False)

    def rest_live(state):
        n, _, least_rest = state
        return jnp.logical_and(n < i, least_rest < SB_DEAD_CARRY)

    def full_visit(state):
        n = state[0]
        return (n + 1,) + visit(i - 1 - n, False, False)

    def top_live(state):
        n, least_top = state
        return jnp.logical_and(n < i, least_top < SB_DEAD_CARRY)

    def top_visit(state):
        n = state[0]
        return n + 1, visit(i - 1 - n, False, True)[0]

    n, least_top, _ = lax.while_loop(rest_live, full_visit, (jnp.int32(0), least_top, least_rest))
    lax.while_loop(top_live, top_visit, (n, least_top))
    for p in range(n_pairs):
        o_ref[:, p * LANES:(p + 1) * LANES] = acc_ref[p].astype(BF16)


def _sb_attention(qb, kb, vb, batch, seq):
    qb = qb.reshape(batch, seq, SB_W)
    kb = kb.reshape(batch, seq, SB_W)
    vb = vb.reshape(batch, seq, SB_W)
    assert TQ_SB == KB_SB
    q_map = lambda b, i: (b, i, 0)
    kv_map = lambda b, i: (b, 0, 0)
    return pl.pallas_call(
        _sb_kernel,
        grid=(batch, seq // TQ_SB),
        in_specs=[
            pl.BlockSpec((None, TQ_SB, SB_W), q_map),
            pl.BlockSpec((None, seq, SB_W), kv_map),
            pl.BlockSpec((None, seq, SB_W), kv_map),
        ],
        out_specs=pl.BlockSpec((None, TQ_SB, SB_W), q_map),
        out_shape=jax.ShapeDtypeStruct((batch, seq, SB_W), BF16),
        scratch_shapes=[
            pltpu.VMEM((SB_HEADS // 2, 2 * TQ_SB, LANES), BF16),
            pltpu.VMEM((SB_HEADS // 2, TQ_SB, LANES), F32),
            pltpu.VMEM((SB_HEADS // 2, 2 * TQ_SB, 1), F32),
        ],
        compiler_params=pltpu.CompilerParams(
            dimension_semantics=("arbitrary", "arbitrary"),
            vmem_limit_bytes=VMEM_LIMIT_BYTES),
        name="sb_attention",
    )(qb, kb, vb)


def _mem_kv_kernel(m_ref, g_ref, w_ref, k_ref, v_ref):
    u = _rms_normalize(m_ref[...], g_ref[...]).astype(BF16)
    k_ref[...] = _dot(u, w_ref[:, :MEM_W]).astype(BF16)
    v_ref[...] = _dot(u, w_ref[:, MEM_W:]).astype(BF16)


def _mem_kv(mem, gain, w_bf16):
    batch, mem_len, _ = mem.shape
    blk = lambda b: (b, 0, 0)
    return pl.pallas_call(
        _mem_kv_kernel,
        grid=(batch,),
        in_specs=[
            pl.BlockSpec((None, mem_len, D_MODEL), blk),
            pl.BlockSpec((1, D_MODEL), lambda b: (0, 0)),
            pl.BlockSpec((D_MODEL, 2 * MEM_W), lambda b: (0, 0)),
        ],
        out_specs=[pl.BlockSpec((None, mem_len, MEM_W), blk)] * 2,
        out_shape=[jax.ShapeDtypeStruct((batch, mem_len, MEM_W), BF16)] * 2,
        compiler_params=pltpu.CompilerParams(
            dimension_semantics=("arbitrary",), vmem_limit_bytes=VMEM_LIMIT_BYTES),
        name="mem_kv",
    )(mem, gain, w_bf16)


_MEM_HEAD_LANES = [slice(h * MEM_HEAD_DIM, (h + 1) * MEM_HEAD_DIM) for h in range(MEM_HEADS)]


def _mem_scores(rows, q_ref, k_ref):
    return [_dot_nt(q_ref[rows, sl], k_ref[:, sl]) for sl in _MEM_HEAD_LANES]


def _mem_values(rows, scores, v_ref, y_ref):
    c = (MEM_HEAD_DIM ** -0.5) * math.log2(math.e)
    for sl, z in zip(_MEM_HEAD_LANES, scores):
        pr = jnp.exp2((z - jnp.max(z, axis=-1, keepdims=True)) * c)
        recip = 1.0 / jnp.sum(pr, axis=-1, keepdims=True)
        y_ref[rows, sl] = (_dot(pr.astype(BF16), v_ref[:, sl]) * recip).astype(BF16)


def _merge_kernel(sink_ref, x_ref, gpre_ref, qa_ref, kp_ref, kc_ref, vp_ref, vc_ref, bias_ref,
                  yb_ref, qm_ref, mk_ref, mv_ref, wgl_ref, wa_ref, wb_ref, wm_ref, wo_ref,
                  gpost_ref, o_ref, qs_ref, ya_ref, ym_ref, *, tiles_per_seq):
    first_of_sequence = pl.program_id(0) % tiles_per_seq == 0
    blocks_per_group = ROW_GROUP // BLOCK
    for g, rows in enumerate(_row_groups(TM_MERGE)):
        blocks = range(g * blocks_per_group, (g + 1) * blocks_per_group)
        swa = [_swa_scores(j, qa_ref, kp_ref, kc_ref, vp_ref, vc_ref, qs_ref) for j in blocks]
        mem = _mem_scores(rows, qm_ref, mk_ref)
        x = x_ref[rows, :]
        u = _gained_bf16(x, gpre_ref[...])
        inv = _inv_rms(x)
        gates = [jax.nn.sigmoid(_dot(u, wgl_ref[:, n * D_MODEL:(n + 1) * D_MODEL]) * inv)
                 for n in range(N_BRANCH)]
        for j, (scores, vband) in zip(blocks, swa):
            _swa_values(j, scores, vband, sink_ref, bias_ref, first_of_sequence, ya_ref)
        _mem_values(rows, mem, mv_ref, ym_ref)
        merged = None
        for gate, y_ref, w_ref in zip(gates, (ya_ref, yb_ref, ym_ref), (wa_ref, wb_ref, wm_ref)):
            term = gate * _dot(y_ref[rows, :], w_ref[...])
            merged = term if merged is None else merged + term
        mix = _dot(merged.astype(BF16), wo_ref[...])
        o_ref[rows, :] = x + _rms_normalize(mix, gpost_ref[...])


def _merge(x2d, g_pre, sinks, bias, qa, ka, va, y_sb, qm, mk, mv, w_gl, w_swa, w_sb, w_mem, w_out,
           g_post, seq):
    t = x2d.shape[0]
    mem_len = mk.shape[1]
    assert seq % TM_MERGE == 0 and TM_MERGE % ROW_GROUP == 0 and ROW_GROUP % BLOCK == 0
    tiles_per_seq = seq // TM_MERGE
    blocks_per_tile = TM_MERGE // BLOCK
    row = lambda i: (i, 0)
    const = lambda i: (0, 0)
    prev = lambda i: (jnp.maximum(i * blocks_per_tile - 1, 0), 0)
    memkv = lambda i: (i // tiles_per_seq, 0, 0)
    resident = functools.partial(pl.BlockSpec, index_map=const, pipeline_mode=pl.Buffered(1))
    return pl.pallas_call(
        functools.partial(_merge_kernel, tiles_per_seq=tiles_per_seq),
        grid=(t // TM_MERGE,),
        in_specs=[
            pl.BlockSpec(memory_space=pltpu.SMEM),
            pl.BlockSpec((TM_MERGE, D_MODEL), row),
            pl.BlockSpec((1, D_MODEL), const),
            pl.BlockSpec((TM_MERGE, SWA_Q_W), row),
            pl.BlockSpec((BLOCK, SWA_KV_W), prev),
            pl.BlockSpec((TM_MERGE, SWA_KV_W), row),
            pl.BlockSpec((BLOCK, SWA_KV_W), prev),
            pl.BlockSpec((TM_MERGE, SWA_KV_W), row),
            pl.BlockSpec((SWA_Q_HEADS, BLOCK, BLOCK), lambda i: (0, 0, 0)),
            pl.BlockSpec((TM_MERGE, SB_W), row),
            pl.BlockSpec((TM_MERGE, MEM_W), row),
            pl.BlockSpec((None, mem_len, MEM_W), memkv),
            pl.BlockSpec((None, mem_len, MEM_W), memkv),
            resident((D_MODEL, N_BRANCH * D_MODEL)),
            resident((SWA_Q_W, D_MODEL)),
            resident((SB_W, D_MODEL)),
            resident((MEM_W, D_MODEL)),
            resident((D_MODEL, D_MODEL)),
            pl.BlockSpec((1, D_MODEL), const),
        ],
        out_specs=pl.BlockSpec((TM_MERGE, D_MODEL), row),
        out_shape=jax.ShapeDtypeStruct((t, D_MODEL), F32),
        scratch_shapes=[
            pltpu.VMEM((blocks_per_tile * SWA_Q_HEADS * BLOCK, LANES), BF16),
            pltpu.VMEM((TM_MERGE, SWA_Q_W), BF16),
            pltpu.VMEM((TM_MERGE, MEM_W), BF16),
        ],
        compiler_params=pltpu.CompilerParams(
            dimension_semantics=("arbitrary",), vmem_limit_bytes=VMEM_LIMIT_BYTES),
        name="merge",
    )(sinks, x2d, g_pre, qa, ka, ka, va, va, bias, y_sb, qm, mk, mv, w_gl, w_swa, w_sb, w_mem,
      w_out, g_post)


def _ffn_kernel(h_ref, gpre_ref, wg_ref, wu_ref, wd_ref, gpost_ref, o_ref):
    for rows in _row_groups(TM_FFN):
        h = h_ref[rows, :]
        u = _gained_bf16(h, gpre_ref[...])
        inv = _inv_rms(h)
        acc = None
        for start, width in FF_CHUNKS:
            gate = _dot(u, wg_ref[:, start:start + width]) * inv
            up = _dot(u, wu_ref[:, start:start + width]) * inv
            act = (gate * jax.nn.sigmoid(gate) * up).astype(BF16)
            part = _dot(act, wd_ref[start:start + width, :])
            acc = part if acc is None else acc + part
        o_ref[rows, :] = h + _rms_normalize(acc, gpost_ref[...])


def _ffn(h2d, g_pre, w_gate, w_up, w_down, g_post):
    t = h2d.shape[0]
    row = lambda i: (i, 0)
    const = lambda i: (0, 0)
    resident = functools.partial(pl.BlockSpec, index_map=const, pipeline_mode=pl.Buffered(1))
    return pl.pallas_call(
        _ffn_kernel,
        grid=(t // TM_FFN,),
        in_specs=[
            pl.BlockSpec((TM_FFN, D_MODEL), row),
            pl.BlockSpec((1, D_MODEL), const),
            resident((D_MODEL, D_FF)),
            resident((D_MODEL, D_FF)),
            resident((D_FF, D_MODEL)),
            pl.BlockSpec((1, D_MODEL), const),
        ],
        out_specs=pl.BlockSpec((TM_FFN, D_MODEL), row),
        out_shape=jax.ShapeDtypeStruct((t, D_MODEL), F32),
        compiler_params=pltpu.CompilerParams(
            dimension_semantics=("arbitrary",), vmem_limit_bytes=VMEM_LIMIT_BYTES),
        name="ffn",
    )(h2d, g_pre, w_gate, w_up, w_down, g_post)


def _prepare_w_in(w_in):
    scale = HEAD_DIM ** -0.5
    assert scale == 0.125
    offs = np.cumsum([0] + [w for _, w in PROJ_PIECES])
    qa = w_in[:, offs[0]:offs[1]].reshape(D_MODEL, SWA_Q_HEADS, HEAD_DIM)
    qa = (qa[:, np.array(SWA_HEAD_ORDER), :] * scale).reshape(D_MODEL, SWA_Q_W)
    qb = w_in[:, offs[3]:offs[4]] * scale
    w_qkv = jnp.concatenate([qa, w_in[:, offs[1]:offs[3]], qb, w_in[:, offs[4]:QKV_W]], axis=1)
    return w_qkv.astype(BF16), w_in[:, QKV_W:].astype(BF16)


def _prepare_w_branch_swa(w):
    w = w.reshape(SWA_Q_HEADS, HEAD_DIM, D_MODEL)[np.array(SWA_HEAD_ORDER)]
    return w.reshape(SWA_Q_W, D_MODEL).astype(BF16)


def kernel(x, mem, ln_mix_pre, ln_mix_post, w_in, swa_sinks, rel_bias, ln_mem, w_mem_kv,
           w_branch_swa, w_branch_sb, w_branch_mem, w_out, ln_ffn_pre, ln_ffn_post,
           w_gate, w_up, w_down):
    batch, seq, d = x.shape
    depth = w_in.shape[0]
    h = x.reshape(batch * seq, d)
    bias = _swa_bias_table(rel_bias)
    for l in range(depth):
        w_qkv, w_gl = _prepare_w_in(w_in[l])
        qa, ka, va, qb, kb, vb, qm = _in_proj(h, ln_mix_pre[l][None], w_qkv)
        y_sb = _sb_attention(qb, kb, vb, batch, seq)
        mk, mv = _mem_kv(mem, ln_mem[l][None], w_mem_kv[l].astype(BF16))
        h = _merge(h, ln_mix_pre[l][None], swa_sinks[l].astype(F32), bias, qa, ka, va,
                   y_sb.reshape(batch * seq, SB_W), qm, mk, mv, w_gl,
                   _prepare_w_branch_swa(w_branch_swa[l]), w_branch_sb[l].astype(BF16),
                   w_branch_mem[l].astype(BF16), w_out[l].astype(BF16), ln_mix_post[l][None], seq)
        h = _ffn(h, ln_ffn_pre[l][None], w_gate[l].astype(BF16), w_up[l].astype(BF16),
                 w_down[l].astype(BF16), ln_ffn_post[l][None])
    return h.reshape(batch, seq, d)
```

```python
import functools
import math

import numpy as np
import jax
import jax.numpy as jnp
from jax import lax
from jax.experimental import pallas as pl
from jax.experimental.pallas import tpu as pltpu

F32 = jnp.float32
BF16 = jnp.bfloat16

D_MODEL = 1024
BLOCK = 128
EPS = 1e-6
HEAD_DIM = 64
SWA_Q_HEADS = 8
SWA_KV_HEADS = 2
SWA_WINDOW = 128
N_BUCKETS = 32
MAX_DISTANCE = 128
SB_HEADS = 8
MEM_HEADS = 4
MEM_HEAD_DIM = 128
SWA_Q_W = SWA_Q_HEADS * HEAD_DIM
SWA_KV_W = SWA_KV_HEADS * HEAD_DIM
SB_W = SB_HEADS * HEAD_DIM
MEM_W = MEM_HEADS * MEM_HEAD_DIM
N_BRANCH = 3
D_FF = 2816
LOG2_E = math.log2(math.e)

LANES = 128
BF16_SUBLANES = 16
VMEM_LIMIT_BYTES = 56 * 1024 * 1024

TM_PROJ = 1024
TM_MERGE = 1024
TM_FFN = 1024
TQ_SB = 256
KB_SB = 256
COL_CHUNK = 512
MEM_KV_BATCHES = 4
ROW_GROUP = 512
SB_DEAD_CARRY = 96.0
SB_TOP_ROWS = 160
FF_CHUNKS = ((0, 1024), (1024, 1024), (2048, 768))

PROJ_PIECES = (("qa", SWA_Q_W), ("ka", SWA_KV_W), ("va", SWA_KV_W), ("qb", SB_W),
               ("kb", SB_W), ("vb", SB_W), ("qm", MEM_W))
QKV_W = sum(w for _, w in PROJ_PIECES)

SWA_HEAD_ORDER = (0, 4, 1, 5, 2, 6, 3, 7)


def _inv_rms(x):
    return lax.rsqrt(jnp.mean(x * x, axis=-1, keepdims=True) + EPS)


def _rms_normalize(x, gain):
    return x * _inv_rms(x) * gain


def _gained_bf16(x, gain):
    return (x * gain).astype(BF16)


def _half_mask(shape, half):
    lane = lax.broadcasted_iota(jnp.int32, shape, len(shape) - 1)
    return (lane < HEAD_DIM) if half == 0 else (lane >= HEAD_DIM)


def _row_groups(rows):
    return [slice(s, s + ROW_GROUP) for s in range(0, rows, ROW_GROUP)]


def _dot_nt(a, b):
    return lax.dot_general(a, b, (((1,), (1,)), ((), ())), preferred_element_type=F32)


def _dot(a, b):
    return jnp.dot(a, b, preferred_element_type=F32)


def _cast_rows_per_step(rows, steps):
    for r in range(BF16_SUBLANES, rows + 1, BF16_SUBLANES):
        if rows % r == 0 and rows // r <= steps:
            return r
    raise ValueError((rows, steps))


def _in_proj_kernel(x_ref, g_ref, w_ref, *refs, casts):
    n_cast = len(casts)
    cast_in, out_refs, cast_out = refs[:n_cast], refs[n_cast:-n_cast], refs[-n_cast:]
    x = x_ref[...]
    u = _gained_bf16(x, g_ref[...])
    inv = _inv_rms(x)
    starts = np.cumsum([0] + [w for _, w in PROJ_PIECES])
    for c0 in range(0, QKV_W, COL_CHUNK):
        c1 = min(c0 + COL_CHUNK, QKV_W)
        y = (_dot(u, w_ref[:, c0:c1]) * inv).astype(BF16)
        for out_ref, p0, p1 in zip(out_refs, starts[:-1], starts[1:]):
            lo, hi = max(c0, p0), min(c1, p1)
            if lo < hi:
                out_ref[:, lo - p0:hi - p0] = y[:, lo - c0:hi - c0]
    for src_ref, dst_ref, col0 in zip(cast_in, cast_out, casts):
        dst_ref[...] = src_ref[:, col0:].astype(BF16)


def _in_proj(x2d, gain, w_bf16, f32_weights):
    t = x2d.shape[0]
    steps = t // TM_PROJ
    in_w = w_bf16.shape[1]
    row = lambda i: (i, 0)
    out_shape = [jax.ShapeDtypeStruct((t, w), BF16) for _, w in PROJ_PIECES]
    out_specs = [pl.BlockSpec((TM_PROJ, w), row) for _, w in PROJ_PIECES]
    cast_in_specs, cast_out_specs, casts = [], [], []
    for w, col0 in f32_weights:
        r = _cast_rows_per_step(w.shape[0], steps)
        n = w.shape[0] // r
        slab = lambda i, n=n: (jnp.minimum(i, n - 1), 0)
        cast_in_specs.append(pl.BlockSpec((r, w.shape[1]), slab))
        cast_out_specs.append(pl.BlockSpec((r, w.shape[1] - col0), slab))
        casts.append(col0)
        out_shape.append(jax.ShapeDtypeStruct((w.shape[0], w.shape[1] - col0), BF16))
    outs = pl.pallas_call(
        functools.partial(_in_proj_kernel, casts=tuple(casts)),
        grid=(steps,),
        in_specs=[
            pl.BlockSpec((TM_PROJ, D_MODEL), row),
            pl.BlockSpec((1, D_MODEL), lambda i: (0, 0)),
            pl.BlockSpec((D_MODEL, in_w), lambda i: (0, 0), pipeline_mode=pl.Buffered(1)),
        ] + cast_in_specs,
        out_specs=out_specs + cast_out_specs,
        out_shape=out_shape,
        compiler_params=pltpu.CompilerParams(
            dimension_semantics=("arbitrary",), vmem_limit_bytes=VMEM_LIMIT_BYTES),
        name="in_proj",
    )(x2d, gain, w_bf16, *[w for w, _ in f32_weights])
    return outs[:len(PROJ_PIECES)], outs[len(PROJ_PIECES):]


def _swa_scores(j, q_ref, kp_ref, kc_ref, vp_ref, vc_ref, qs_ref):
    half0 = _half_mask((BLOCK, LANES), 0)
    rows = SWA_Q_HEADS * BLOCK
    blk = slice(j * BLOCK, (j + 1) * BLOCK)
    for p in range(SWA_Q_HEADS // 2):
        q2 = q_ref[blk, p * LANES:(p + 1) * LANES]
        zero = jnp.zeros_like(q2)
        base = j * rows + 2 * p * BLOCK
        qs_ref[base:base + BLOCK, :] = jnp.where(half0, q2, zero)
        qs_ref[base + BLOCK:base + 2 * BLOCK, :] = jnp.where(half0, zero, q2)
    if j == 0:
        kband = jnp.concatenate([kp_ref[...], kc_ref[blk, :]], axis=0)
        vband = jnp.concatenate([vp_ref[...], vc_ref[blk, :]], axis=0)
    else:
        band = slice((j - 1) * BLOCK, (j + 1) * BLOCK)
        kband, vband = kc_ref[band, :], vc_ref[band, :]
    return _dot_nt(kband, qs_ref[j * rows:(j + 1) * rows, :]), vband


def _swa_values(j, scores, vband, sink_ref, bias_ref, first_of_sequence, y_ref):
    key = lax.broadcasted_iota(jnp.int32, (BLOCK, BLOCK), 0)
    qry = lax.broadcasted_iota(jnp.int32, (BLOCK, BLOCK), 1)
    from_prev = key > qry
    first_kv = key < HEAD_DIM
    weights, recips = [], []
    for h in range(SWA_Q_HEADS):
        head = SWA_HEAD_ORDER[h]
        zh = scores[:, h * BLOCK:(h + 1) * BLOCK]
        s = jnp.where(from_prev, zh[:BLOCK], zh[BLOCK:]) + bias_ref[head]
        if j == 0:
            s = jnp.where(jnp.logical_and(from_prev, first_of_sequence), -jnp.inf, s)
        sink = sink_ref[head]
        m = jnp.maximum(jnp.max(s, axis=0, keepdims=True), sink)
        pr = jnp.exp(s - m)
        denom = jnp.sum(pr, axis=0, keepdims=True) + jnp.exp(sink - m)
        recips.append(1.0 / denom)
        zero = jnp.zeros_like(pr)
        unfolded = jnp.concatenate(
            [jnp.where(from_prev, pr, zero), jnp.where(from_prev, zero, pr)], axis=0)
        weights.append(unfolded.astype(BF16))
    v_t = vband.astype(F32).T.astype(BF16)
    o_t = _dot(v_t, jnp.concatenate(weights, axis=1))
    for p in range(SWA_Q_HEADS // 2):
        t0 = o_t[:, (2 * p) * BLOCK:(2 * p + 1) * BLOCK] * recips[2 * p]
        t1 = o_t[:, (2 * p + 1) * BLOCK:(2 * p + 2) * BLOCK] * recips[2 * p + 1]
        tile = jnp.where(first_kv, t0, t1).T
        y_ref[j * BLOCK:(j + 1) * BLOCK, p * LANES:(p + 1) * LANES] = tile.astype(BF16)


def _t5_bucket(dist):
    max_exact = N_BUCKETS // 2
    d = np.maximum(dist, 0)
    df = np.maximum(d, 1).astype(np.float32)
    large = max_exact + (np.log(df / np.float32(max_exact))
                         / np.float32(math.log(MAX_DISTANCE / max_exact))
                         * np.float32(N_BUCKETS - max_exact)).astype(np.int32)
    large = np.minimum(large, N_BUCKETS - 1)
    return np.where(d < max_exact, d, large)


def _swa_bias_table(rel_bias):
    c = np.arange(BLOCK)[:, None]
    r = np.arange(BLOCK)[None, :]
    dist = np.where(c > r, r + BLOCK - c, r - c)
    assert SWA_WINDOW == BLOCK
    bucket = _t5_bucket(dist)
    rb = rel_bias.astype(F32)
    bias = jnp.zeros((SWA_Q_HEADS, BLOCK, BLOCK), F32)
    for b in range(N_BUCKETS):
        bias = jnp.where((bucket == b)[None], rb[b][:, None, None], bias)
    return bias


def _sb_kernel(q_ref, k_ref, v_ref, o_ref, qs_ref, acc_ref, carry_ref):
    i = pl.program_id(1)
    n_pairs = SB_HEADS // 2
    top, rest = SB_TOP_ROWS, TQ_SB - SB_TOP_ROWS
    rows = 2 * TQ_SB
    half0_top = _half_mask((top, LANES), 0)
    half0_rest = _half_mask((rest, LANES), 0)
    r = lax.broadcasted_iota(jnp.int32, (KB_SB, KB_SB), 0)
    c = lax.broadcasted_iota(jnp.int32, (KB_SB, KB_SB), 1)
    suffix = jnp.where(r > c, 1.0, 0.0).astype(BF16)
    rr = lax.broadcasted_iota(jnp.int32, (rows, KB_SB), 0)
    cc = lax.broadcasted_iota(jnp.int32, (rows, KB_SB), 1)
    tile_row = jnp.where(rr < top, rr, jnp.where(rr < 2 * top + rest, rr - top, rr - top - rest))
    causal = cc < tile_row

    for p in range(n_pairs):
        q2 = q_ref[:, p * LANES:(p + 1) * LANES]
        q_top, q_rest = q2[:top], q2[top:]
        qs_ref[p, 0:top, :] = jnp.where(half0_top, q_top, jnp.zeros_like(q_top))
        qs_ref[p, top:2 * top, :] = jnp.where(half0_top, jnp.zeros_like(q_top), q_top)
        qs_ref[p, 2 * top:2 * top + rest, :] = jnp.where(half0_rest, q_rest, jnp.zeros_like(q_rest))
        qs_ref[p, 2 * top + rest:, :] = jnp.where(half0_rest, jnp.zeros_like(q_rest), q_rest)
    lanes = [slice(p * LANES, (p + 1) * LANES) for p in range(n_pairs)]

    def visit(j, diagonal, top_only):
        n = 2 * top if top_only else rows
        start = pl.multiple_of(j * KB_SB, KB_SB)
        zs = [_dot_nt(qs_ref[p, 0:n, :], k_ref[pl.ds(start, KB_SB), lanes[p]])
              for p in range(n_pairs)]
        sps, lzs, firsts = [], [], []
        for z in zs:
            sp = jnp.maximum(z, 0.0) + jnp.log(1.0 + jnp.exp2(jnp.abs(z) * -LOG2_E))
            lz = z - sp
            if diagonal:
                sp = jnp.where(causal, sp, 0.0)
                lz = jnp.where(causal, lz, -jnp.inf)
            sps.append(sp.astype(BF16))
            lzs.append(lz)
            firsts.append(sp[:, 0:1])
        within_all = _dot(jnp.concatenate(sps, axis=0), suffix)
        least = None
        for p in range(n_pairs):
            within = within_all[p * n:(p + 1) * n]
            total = within[:, 0:1] + firsts[p]
            if diagonal:
                a = jnp.exp(lzs[p] - within)
            else:
                carry = carry_ref[p, 0:n, :]
                a = jnp.exp(lzs[p] - (within + carry))
                total = carry + total
            least = total if least is None else jnp.minimum(least, total)
            pv = _dot(a.astype(BF16), v_ref[pl.ds(start, KB_SB), lanes[p]])
            upd_top = jnp.where(half0_top, pv[0:top], pv[top:2 * top])
            if top_only:
                acc_ref[p, 0:top, :] += upd_top
            else:
                upd_rest = jnp.where(half0_rest, pv[2 * top:2 * top + rest], pv[2 * top + rest:])
                upd = jnp.concatenate([upd_top, upd_rest], axis=0)
                if diagonal:
                    acc_ref[p] = upd
                else:
                    acc_ref[p] += upd
            carry_ref[p, 0:n, :] = total
        least_top = jnp.min(least[0:2 * top])
        least_rest = jnp.float32(jnp.inf) if top_only else jnp.min(least[2 * top:])
        return least_top, least_rest

    least_top, least_rest = visit(i, True, False)

    def rest_live(state):
        n, _, least_rest = state
        return jnp.logical_and(n < i, least_rest < SB_DEAD_CARRY)

    def full_visit(state):
        n = state[0]
        return (n + 1,) + visit(i - 1 - n, False, False)

    def top_live(state):
        n, least_top = state
        return jnp.logical_and(n < i, least_top < SB_DEAD_CARRY)

    def top_visit(state):
        n = state[0]
        return n + 1, visit(i - 1 - n, False, True)[0]

    n, least_top, _ = lax.while_loop(rest_live, full_visit, (jnp.int32(0), least_top, least_rest))
    lax.while_loop(top_live, top_visit, (n, least_top))
    for p in range(n_pairs):
        o_ref[:, p * LANES:(p + 1) * LANES] = acc_ref[p].astype(BF16)


def _sb_attention(qb, kb, vb, batch, seq):
    qb = qb.reshape(batch, seq, SB_W)
    kb = kb.reshape(batch, seq, SB_W)
    vb = vb.reshape(batch, seq, SB_W)
    assert TQ_SB == KB_SB
    q_map = lambda b, i: (b, i, 0)
    kv_map = lambda b, i: (b, 0, 0)
    return pl.pallas_call(
        _sb_kernel,
        grid=(batch, seq // TQ_SB),
        in_specs=[
            pl.BlockSpec((None, TQ_SB, SB_W), q_map),
            pl.BlockSpec((None, seq, SB_W), kv_map),
            pl.BlockSpec((None, seq, SB_W), kv_map),
        ],
        out_specs=pl.BlockSpec((None, TQ_SB, SB_W), q_map),
        out_shape=jax.ShapeDtypeStruct((batch, seq, SB_W), BF16),
        scratch_shapes=[
            pltpu.VMEM((SB_HEADS // 2, 2 * TQ_SB, LANES), BF16),
            pltpu.VMEM((SB_HEADS // 2, TQ_SB, LANES), F32),
            pltpu.VMEM((SB_HEADS // 2, 2 * TQ_SB, 1), F32),
        ],
        compiler_params=pltpu.CompilerParams(
            dimension_semantics=("arbitrary", "arbitrary"),
            vmem_limit_bytes=VMEM_LIMIT_BYTES),
        name="sb_attention",
    )(qb, kb, vb)


def _mem_kv_kernel(m_ref, g_ref, w_ref, k_ref, v_ref):
    nb, mem_len, _ = m_ref.shape
    m = m_ref[...].reshape(nb * mem_len, D_MODEL)
    u = _rms_normalize(m, g_ref[...]).astype(BF16)
    k_ref[...] = _dot(u, w_ref[:, :MEM_W]).astype(BF16).reshape(nb, mem_len, MEM_W)
    v_ref[...] = _dot(u, w_ref[:, MEM_W:]).astype(BF16).reshape(nb, mem_len, MEM_W)


def _mem_kv(mem, gain, w_bf16):
    batch, mem_len, _ = mem.shape
    nb = MEM_KV_BATCHES
    assert batch % nb == 0
    blk = lambda b: (b, 0, 0)
    return pl.pallas_call(
        _mem_kv_kernel,
        grid=(batch // nb,),
        in_specs=[
            pl.BlockSpec((nb, mem_len, D_MODEL), blk),
            pl.BlockSpec((1, D_MODEL), lambda b: (0, 0)),
            pl.BlockSpec((D_MODEL, 2 * MEM_W), lambda b: (0, 0)),
        ],
        out_specs=[pl.BlockSpec((nb, mem_len, MEM_W), blk)] * 2,
        out_shape=[jax.ShapeDtypeStruct((batch, mem_len, MEM_W), BF16)] * 2,
        compiler_params=pltpu.CompilerParams(
            dimension_semantics=("arbitrary",), vmem_limit_bytes=VMEM_LIMIT_BYTES),
        name="mem_kv",
    )(mem, gain, w_bf16)


_MEM_HEAD_LANES = [slice(h * MEM_HEAD_DIM, (h + 1) * MEM_HEAD_DIM) for h in range(MEM_HEADS)]


def _mem_scores(rows, q_ref, k_ref):
    return [_dot_nt(q_ref[rows, sl], k_ref[:, sl]) for sl in _MEM_HEAD_LANES]


def _mem_values(rows, scores, v_ref, y_ref):
    c = (MEM_HEAD_DIM ** -0.5) * math.log2(math.e)
    for sl, z in zip(_MEM_HEAD_LANES, scores):
        pr = jnp.exp2((z - jnp.max(z, axis=-1, keepdims=True)) * c)
        recip = 1.0 / jnp.sum(pr, axis=-1, keepdims=True)
        y_ref[rows, sl] = (_dot(pr.astype(BF16), v_ref[:, sl]) * recip).astype(BF16)


def _merge_kernel(sink_ref, x_ref, gpre_ref, qa_ref, kp_ref, kc_ref, vp_ref, vc_ref, bias_ref,
                  yb_ref, qm_ref, mk_ref, mv_ref, wgl_ref, wa_ref, wb_ref, wm_ref, wo_ref,
                  gpost_ref, o_ref, qs_ref, ya_ref, ym_ref, *, tiles_per_seq):
    first_of_sequence = pl.program_id(0) % tiles_per_seq == 0
    blocks_per_group = ROW_GROUP // BLOCK
    for g, rows in enumerate(_row_groups(TM_MERGE)):
        blocks = range(g * blocks_per_group, (g + 1) * blocks_per_group)
        swa = [_swa_scores(j, qa_ref, kp_ref, kc_ref, vp_ref, vc_ref, qs_ref) for j in blocks]
        mem = _mem_scores(rows, qm_ref, mk_ref)
        x = x_ref[rows, :]
        u = _gained_bf16(x, gpre_ref[...])
        inv = _inv_rms(x)
        gates = [jax.nn.sigmoid(_dot(u, wgl_ref[:, n * D_MODEL:(n + 1) * D_MODEL]) * inv)
                 for n in range(N_BRANCH)]
        for j, (scores, vband) in zip(blocks, swa):
            _swa_values(j, scores, vband, sink_ref, bias_ref, first_of_sequence, ya_ref)
        _mem_values(rows, mem, mv_ref, ym_ref)
        merged = None
        for gate, y_ref, w_ref in zip(gates, (ya_ref, yb_ref, ym_ref), (wa_ref, wb_ref, wm_ref)):
            term = gate * _dot(y_ref[rows, :], w_ref[...])
            merged = term if merged is None else merged + term
        mix = _dot(merged.astype(BF16), wo_ref[...])
        o_ref[rows, :] = x + _rms_normalize(mix, gpost_ref[...])


def _merge(x2d, g_pre, sinks, bias, qa, ka, va, y_sb, qm, mk, mv, w_gl, w_swa, w_sb, w_mem, w_out,
           g_post, seq):
    t = x2d.shape[0]
    mem_len = mk.shape[1]
    assert seq % TM_MERGE == 0 and TM_MERGE % ROW_GROUP == 0 and ROW_GROUP % BLOCK == 0
    tiles_per_seq = seq // TM_MERGE
    blocks_per_tile = TM_MERGE // BLOCK
    row = lambda i: (i, 0)
    const = lambda i: (0, 0)
    prev = lambda i: (jnp.maximum(i * blocks_per_tile - 1, 0), 0)
    memkv = lambda i: (i // tiles_per_seq, 0, 0)
    resident = functools.partial(pl.BlockSpec, index_map=const, pipeline_mode=pl.Buffered(1))
    return pl.pallas_call(
        functools.partial(_merge_kernel, tiles_per_seq=tiles_per_seq),
        grid=(t // TM_MERGE,),
        in_specs=[
            pl.BlockSpec(memory_space=pltpu.SMEM),
            pl.BlockSpec((TM_MERGE, D_MODEL), row),
            pl.BlockSpec((1, D_MODEL), const),
            pl.BlockSpec((TM_MERGE, SWA_Q_W), row),
            pl.BlockSpec((BLOCK, SWA_KV_W), prev),
            pl.BlockSpec((TM_MERGE, SWA_KV_W), row),
            pl.BlockSpec((BLOCK, SWA_KV_W), prev),
            pl.BlockSpec((TM_MERGE, SWA_KV_W), row),
            pl.BlockSpec((SWA_Q_HEADS, BLOCK, BLOCK), lambda i: (0, 0, 0)),
            pl.BlockSpec((TM_MERGE, SB_W), row),
            pl.BlockSpec((TM_MERGE, MEM_W), row),
            pl.BlockSpec((None, mem_len, MEM_W), memkv),
            pl.BlockSpec((None, mem_len, MEM_W), memkv),
            resident((D_MODEL, N_BRANCH * D_MODEL)),
            resident((SWA_Q_W, D_MODEL)),
            resident((SB_W, D_MODEL)),
            resident((MEM_W, D_MODEL)),
            resident((D_MODEL, D_MODEL)),
            pl.BlockSpec((1, D_MODEL), const),
        ],
        out_specs=pl.BlockSpec((TM_MERGE, D_MODEL), row),
        out_shape=jax.ShapeDtypeStruct((t, D_MODEL), F32),
        scratch_shapes=[
            pltpu.VMEM((blocks_per_tile * SWA_Q_HEADS * BLOCK, LANES), BF16),
            pltpu.VMEM((TM_MERGE, SWA_Q_W), BF16),
            pltpu.VMEM((TM_MERGE, MEM_W), BF16),
        ],
        compiler_params=pltpu.CompilerParams(
            dimension_semantics=("arbitrary",), vmem_limit_bytes=VMEM_LIMIT_BYTES),
        name="merge",
    )(sinks, x2d, g_pre, qa, ka, ka, va, va, bias, y_sb, qm, mk, mv, w_gl, w_swa, w_sb, w_mem,
      w_out, g_post)


def _ffn_kernel(h_ref, gpre_ref, wg_ref, wu_ref, wd_ref, gpost_ref, o_ref):
    for rows in _row_groups(TM_FFN):
        h = h_ref[rows, :]
        u = _gained_bf16(h, gpre_ref[...])
        inv = _inv_rms(h)
        acc = None
        for start, width in FF_CHUNKS:
            gate = _dot(u, wg_ref[:, start:start + width]) * inv
            up = _dot(u, wu_ref[:, start:start + width]) * inv
            act = (gate * jax.nn.sigmoid(gate) * up).astype(BF16)
            part = _dot(act, wd_ref[start:start + width, :])
            acc = part if acc is None else acc + part
        o_ref[rows, :] = h + _rms_normalize(acc, gpost_ref[...])


def _ffn(h2d, g_pre, w_gate, w_up, w_down, g_post):
    t = h2d.shape[0]
    row = lambda i: (i, 0)
    const = lambda i: (0, 0)
    resident = functools.partial(pl.BlockSpec, index_map=const, pipeline_mode=pl.Buffered(1))
    return pl.pallas_call(
        _ffn_kernel,
        grid=(t // TM_FFN,),
        in_specs=[
            pl.BlockSpec((TM_FFN, D_MODEL), row),
            pl.BlockSpec((1, D_MODEL), const),
            resident((D_MODEL, D_FF)),
            resident((D_MODEL, D_FF)),
            resident((D_FF, D_MODEL)),
            pl.BlockSpec((1, D_MODEL), const),
        ],
        out_specs=pl.BlockSpec((TM_FFN, D_MODEL), row),
        out_shape=jax.ShapeDtypeStruct((t, D_MODEL), F32),
        compiler_params=pltpu.CompilerParams(
            dimension_semantics=("arbitrary",), vmem_limit_bytes=VMEM_LIMIT_BYTES),
        name="ffn",
    )(h2d, g_pre, w_gate, w_up, w_down, g_post)


def _prepare_w_in(w_in):
    scale = HEAD_DIM ** -0.5
    assert scale == 0.125
    offs = np.cumsum([0] + [w for _, w in PROJ_PIECES])
    qa = w_in[:, offs[0]:offs[1]].reshape(D_MODEL, SWA_Q_HEADS, HEAD_DIM)
    qa = (qa[:, np.array(SWA_HEAD_ORDER), :] * scale).reshape(D_MODEL, SWA_Q_W)
    qb = w_in[:, offs[3]:offs[4]] * scale
    w_qkv = jnp.concatenate([qa, w_in[:, offs[1]:offs[3]], qb, w_in[:, offs[4]:QKV_W]], axis=1)
    return w_qkv.astype(BF16)


def _prepare_w_branch_swa(w):
    w = w.reshape(SWA_Q_HEADS, HEAD_DIM, D_MODEL)[np.array(SWA_HEAD_ORDER)]
    return w.reshape(SWA_Q_W, D_MODEL)


def kernel(x, mem, ln_mix_pre, ln_mix_post, w_in, swa_sinks, rel_bias, ln_mem, w_mem_kv,
           w_branch_swa, w_branch_sb, w_branch_mem, w_out, ln_ffn_pre, ln_ffn_post,
           w_gate, w_up, w_down):
    batch, seq, d = x.shape
    depth = w_in.shape[0]
    h = x.reshape(batch * seq, d)
    bias = _swa_bias_table(rel_bias)
    for l in range(depth):
        later = ((w_in[l], QKV_W), (_prepare_w_branch_swa(w_branch_swa[l]), 0),
                 (w_branch_sb[l], 0), (w_branch_mem[l], 0), (w_out[l], 0), (w_mem_kv[l], 0),
                 (w_gate[l], 0), (w_up[l], 0), (w_down[l], 0))
        (qa, ka, va, qb, kb, vb, qm), later_bf16 = _in_proj(
            h, ln_mix_pre[l][None], _prepare_w_in(w_in[l]), later)
        w_gl, w_swa, w_sb, w_mem, w_o, w_mkv, w_g, w_u, w_d = later_bf16
        y_sb = _sb_attention(qb, kb, vb, batch, seq)
        mk, mv = _mem_kv(mem, ln_mem[l][None], w_mkv)
        h = _merge(h, ln_mix_pre[l][None], swa_sinks[l].astype(F32), bias, qa, ka, va,
                   y_sb.reshape(batch * seq, SB_W), qm, mk, mv, w_gl, w_swa, w_sb, w_mem, w_o,
                   ln_mix_post[l][None], seq)
        h = _ffn(h, ln_ffn_pre[l][None], w_g, w_u, w_d, ln_ffn_post[l][None])
    return h.reshape(batch, seq, d)
```

```python
import functools
import math

import numpy as np
import jax
import jax.numpy as jnp
from jax import lax
from jax.experimental import pallas as pl
from jax.experimental.pallas import tpu as pltpu

F32 = jnp.float32
BF16 = jnp.bfloat16

D_MODEL = 1024
BLOCK = 128
EPS = 1e-6
HEAD_DIM = 64
SWA_Q_HEADS = 8
SWA_KV_HEADS = 2
SWA_WINDOW = 128
N_BUCKETS = 32
MAX_DISTANCE = 128
SB_HEADS = 8
MEM_HEADS = 4
MEM_HEAD_DIM = 128
SWA_Q_W = SWA_Q_HEADS * HEAD_DIM
SWA_KV_W = SWA_KV_HEADS * HEAD_DIM
SB_W = SB_HEADS * HEAD_DIM
MEM_W = MEM_HEADS * MEM_HEAD_DIM
N_BRANCH = 3
D_FF = 2816
LOG2_E = math.log2(math.e)

LANES = 128
BF16_SUBLANES = 16
VMEM_LIMIT_BYTES = 56 * 1024 * 1024

TM_PROJ = 1024
TM_MERGE = 1024
TM_FFN = 1024
TQ_SB = 256
KB_SB = 256
COL_CHUNK = 512
MEM_KV_BATCHES = 4
ROW_GROUP_PROJ = 512
ROW_GROUP_MERGE = 512
ROW_GROUP_FFN = 256
SB_DEAD_CARRY = 96.0
SB_TOP_ROWS = 144
FF_CHUNKS = ((0, 1024), (1024, 1024), (2048, 768))

PROJ_PIECES = (("qa", SWA_Q_W), ("ka", SWA_KV_W), ("va", SWA_KV_W), ("qb", SB_W),
               ("kb", SB_W), ("vb", SB_W), ("qm", MEM_W))
QKV_W = sum(w for _, w in PROJ_PIECES)

SWA_HEAD_ORDER = (0, 4, 1, 5, 2, 6, 3, 7)


def _inv_rms(x):
    return lax.rsqrt(jnp.mean(x * x, axis=-1, keepdims=True) + EPS)


def _rms_normalize(x, gain):
    return x * _inv_rms(x) * gain


def _gained_bf16(x, gain):
    return (x * gain).astype(BF16)


def _half_mask(shape, half):
    lane = lax.broadcasted_iota(jnp.int32, shape, len(shape) - 1)
    return (lane < HEAD_DIM) if half == 0 else (lane >= HEAD_DIM)


def _row_groups(rows, group):
    return [slice(s, s + group) for s in range(0, rows, group)]


def _dot_nt(a, b):
    return lax.dot_general(a, b, (((1,), (1,)), ((), ())), preferred_element_type=F32)


def _dot(a, b):
    return jnp.dot(a, b, preferred_element_type=F32)


def _cast_rows_per_step(rows, steps):
    for r in range(BF16_SUBLANES, rows + 1, BF16_SUBLANES):
        if rows % r == 0 and rows // r <= steps:
            return r
    raise ValueError((rows, steps))


def _in_proj_kernel(x_ref, g_ref, w_ref, *refs, casts):
    n_cast = len(casts)
    cast_in, out_refs, cast_out = refs[:n_cast], refs[n_cast:-n_cast], refs[-n_cast:]
    starts = np.cumsum([0] + [w for _, w in PROJ_PIECES])
    for rows in _row_groups(TM_PROJ, ROW_GROUP_PROJ):
        x = x_ref[rows, :]
        u = _gained_bf16(x, g_ref[...])
        inv = _inv_rms(x)
        for c0 in range(0, QKV_W, COL_CHUNK):
            c1 = min(c0 + COL_CHUNK, QKV_W)
            y = (_dot(u, w_ref[:, c0:c1]) * inv).astype(BF16)
            for out_ref, p0, p1 in zip(out_refs, starts[:-1], starts[1:]):
                lo, hi = max(c0, p0), min(c1, p1)
                if lo < hi:
                    out_ref[rows, lo - p0:hi - p0] = y[:, lo - c0:hi - c0]
    for src_ref, dst_ref, col0 in zip(cast_in, cast_out, casts):
        dst_ref[...] = src_ref[:, col0:].astype(BF16)


def _in_proj(x2d, gain, w_bf16, f32_weights):
    t = x2d.shape[0]
    steps = t // TM_PROJ
    in_w = w_bf16.shape[1]
    row = lambda i: (i, 0)
    out_shape = [jax.ShapeDtypeStruct((t, w), BF16) for _, w in PROJ_PIECES]
    out_specs = [pl.BlockSpec((TM_PROJ, w), row) for _, w in PROJ_PIECES]
    cast_in_specs, cast_out_specs, casts = [], [], []
    for w, col0 in f32_weights:
        r = _cast_rows_per_step(w.shape[0], steps)
        n = w.shape[0] // r
        slab = lambda i, n=n: (jnp.minimum(i, n - 1), 0)
        cast_in_specs.append(pl.BlockSpec((r, w.shape[1]), slab))
        cast_out_specs.append(pl.BlockSpec((r, w.shape[1] - col0), slab))
        casts.append(col0)
        out_shape.append(jax.ShapeDtypeStruct((w.shape[0], w.shape[1] - col0), BF16))
    outs = pl.pallas_call(
        functools.partial(_in_proj_kernel, casts=tuple(casts)),
        grid=(steps,),
        in_specs=[
            pl.BlockSpec((TM_PROJ, D_MODEL), row),
            pl.BlockSpec((1, D_MODEL), lambda i: (0, 0)),
            pl.BlockSpec((D_MODEL, in_w), lambda i: (0, 0), pipeline_mode=pl.Buffered(1)),
        ] + cast_in_specs,
        out_specs=out_specs + cast_out_specs,
        out_shape=out_shape,
        compiler_params=pltpu.CompilerParams(
            dimension_semantics=("arbitrary",), vmem_limit_bytes=VMEM_LIMIT_BYTES),
        name="in_proj",
    )(x2d, gain, w_bf16, *[w for w, _ in f32_weights])
    return outs[:len(PROJ_PIECES)], outs[len(PROJ_PIECES):]


def _swa_scores(j, q_ref, kp_ref, kc_ref, vp_ref, vc_ref, qs_ref):
    half0 = _half_mask((BLOCK, LANES), 0)
    rows = SWA_Q_HEADS * BLOCK
    blk = slice(j * BLOCK, (j + 1) * BLOCK)
    for p in range(SWA_Q_HEADS // 2):
        q2 = q_ref[blk, p * LANES:(p + 1) * LANES]
        zero = jnp.zeros_like(q2)
        base = j * rows + 2 * p * BLOCK
        qs_ref[base:base + BLOCK, :] = jnp.where(half0, q2, zero)
        qs_ref[base + BLOCK:base + 2 * BLOCK, :] = jnp.where(half0, zero, q2)
    if j == 0:
        kband = jnp.concatenate([kp_ref[...], kc_ref[blk, :]], axis=0)
        vband = jnp.concatenate([vp_ref[...], vc_ref[blk, :]], axis=0)
    else:
        band = slice((j - 1) * BLOCK, (j + 1) * BLOCK)
        kband, vband = kc_ref[band, :], vc_ref[band, :]
    return _dot_nt(kband, qs_ref[j * rows:(j + 1) * rows, :]), vband


def _swa_values(j, scores, vband, sink_ref, bias_ref, first_of_sequence, y_ref):
    key = lax.broadcasted_iota(jnp.int32, (BLOCK, BLOCK), 0)
    qry = lax.broadcasted_iota(jnp.int32, (BLOCK, BLOCK), 1)
    from_prev = key > qry
    first_kv = key < HEAD_DIM
    weights, recips = [], []
    for h in range(SWA_Q_HEADS):
        head = SWA_HEAD_ORDER[h]
        zh = scores[:, h * BLOCK:(h + 1) * BLOCK]
        s = jnp.where(from_prev, zh[:BLOCK], zh[BLOCK:]) + bias_ref[head]
        if j == 0:
            s = jnp.where(jnp.logical_and(from_prev, first_of_sequence), -jnp.inf, s)
        sink = sink_ref[head]
        m = jnp.maximum(jnp.max(s, axis=0, keepdims=True), sink)
        pr = jnp.exp(s - m)
        denom = jnp.sum(pr, axis=0, keepdims=True) + jnp.exp(sink - m)
        recips.append(1.0 / denom)
        zero = jnp.zeros_like(pr)
        unfolded = jnp.concatenate(
            [jnp.where(from_prev, pr, zero), jnp.where(from_prev, zero, pr)], axis=0)
        weights.append(unfolded.astype(BF16))
    v_t = vband.astype(F32).T.astype(BF16)
    o_t = _dot(v_t, jnp.concatenate(weights, axis=1))
    for p in range(SWA_Q_HEADS // 2):
        t0 = o_t[:, (2 * p) * BLOCK:(2 * p + 1) * BLOCK] * recips[2 * p]
        t1 = o_t[:, (2 * p + 1) * BLOCK:(2 * p + 2) * BLOCK] * recips[2 * p + 1]
        tile = jnp.where(first_kv, t0, t1).T
        y_ref[j * BLOCK:(j + 1) * BLOCK, p * LANES:(p + 1) * LANES] = tile.astype(BF16)


def _t5_bucket(dist):
    max_exact = N_BUCKETS // 2
    d = np.maximum(dist, 0)
    df = np.maximum(d, 1).astype(np.float32)
    large = max_exact + (np.log(df / np.float32(max_exact))
                         / np.float32(math.log(MAX_DISTANCE / max_exact))
                         * np.float32(N_BUCKETS - max_exact)).astype(np.int32)
    large = np.minimum(large, N_BUCKETS - 1)
    return np.where(d < max_exact, d, large)


def _swa_bias_table(rel_bias):
    c = np.arange(BLOCK)[:, None]
    r = np.arange(BLOCK)[None, :]
    dist = np.where(c > r, r + BLOCK - c, r - c)
    assert SWA_WINDOW == BLOCK
    bucket = _t5_bucket(dist)
    rb = rel_bias.astype(F32)
    bias = jnp.zeros((SWA_Q_HEADS, BLOCK, BLOCK), F32)
    for b in range(N_BUCKETS):
        bias = jnp.where((bucket == b)[None], rb[b][:, None, None], bias)
    return bias


def _sb_kernel(q_ref, k_ref, v_ref, o_ref, qs_ref, acc_ref, carry_ref):
    i = pl.program_id(1)
    n_pairs = SB_HEADS // 2
    top, rest = SB_TOP_ROWS, TQ_SB - SB_TOP_ROWS
    rows = 2 * TQ_SB
    half0_top = _half_mask((top, LANES), 0)
    half0_rest = _half_mask((rest, LANES), 0)
    r = lax.broadcasted_iota(jnp.int32, (KB_SB, KB_SB), 0)
    c = lax.broadcasted_iota(jnp.int32, (KB_SB, KB_SB), 1)
    suffix = jnp.where(r > c, 1.0, 0.0).astype(BF16)
    rr = lax.broadcasted_iota(jnp.int32, (rows, KB_SB), 0)
    cc = lax.broadcasted_iota(jnp.int32, (rows, KB_SB), 1)
    tile_row = jnp.where(rr < top, rr, jnp.where(rr < 2 * top + rest, rr - top, rr - top - rest))
    causal = cc < tile_row

    for p in range(n_pairs):
        q2 = q_ref[:, p * LANES:(p + 1) * LANES]
        q_top, q_rest = q2[:top], q2[top:]
        qs_ref[p, 0:top, :] = jnp.where(half0_top, q_top, jnp.zeros_like(q_top))
        qs_ref[p, top:2 * top, :] = jnp.where(half0_top, jnp.zeros_like(q_top), q_top)
        qs_ref[p, 2 * top:2 * top + rest, :] = jnp.where(half0_rest, q_rest, jnp.zeros_like(q_rest))
        qs_ref[p, 2 * top + rest:, :] = jnp.where(half0_rest, jnp.zeros_like(q_rest), q_rest)
    lanes = [slice(p * LANES, (p + 1) * LANES) for p in range(n_pairs)]

    def visit(j, diagonal, top_only):
        n = 2 * top if top_only else rows
        start = pl.multiple_of(j * KB_SB, KB_SB)
        zs = [_dot_nt(qs_ref[p, 0:n, :], k_ref[pl.ds(start, KB_SB), lanes[p]])
              for p in range(n_pairs)]
        sps, lzs, firsts = [], [], []
        for z in zs:
            sp = jnp.maximum(z, 0.0) + jnp.log(1.0 + jnp.exp2(jnp.abs(z) * -LOG2_E))
            lz = z - sp
            if diagonal:
                sp = jnp.where(causal, sp, 0.0)
                lz = jnp.where(causal, lz, -jnp.inf)
            sps.append(sp.astype(BF16))
            lzs.append(lz)
            firsts.append(sp[:, 0:1])
        within_all = _dot(jnp.concatenate(sps, axis=0), suffix)
        least = None
        for p in range(n_pairs):
            within = within_all[p * n:(p + 1) * n]
            total = within[:, 0:1] + firsts[p]
            if diagonal:
                a = jnp.exp(lzs[p] - within)
            else:
                carry = carry_ref[p, 0:n, :]
                a = jnp.exp(lzs[p] - (within + carry))
                total = carry + total
            least = total if least is None else jnp.minimum(least, total)
            pv = _dot(a.astype(BF16), v_ref[pl.ds(start, KB_SB), lanes[p]])
            upd_top = jnp.where(half0_top, pv[0:top], pv[top:2 * top])
            if top_only:
                acc_ref[p, 0:top, :] += upd_top
            else:
                upd_rest = jnp.where(half0_rest, pv[2 * top:2 * top + rest], pv[2 * top + rest:])
                upd = jnp.concatenate([upd_top, upd_rest], axis=0)
                if diagonal:
                    acc_ref[p] = upd
                else:
                    acc_ref[p] += upd
            carry_ref[p, 0:n, :] = total
        least_top = jnp.min(least[0:2 * top])
        least_rest = jnp.float32(jnp.inf) if top_only else jnp.min(least[2 * top:])
        return least_top, least_rest

    least_top, least_rest = visit(i, True, False)

    def rest_live(state):
        n, _, least_rest = state
        return jnp.logical_and(n < i, least_rest < SB_DEAD_CARRY)

    def full_visit(state):
        n = state[0]
        return (n + 1,) + visit(i - 1 - n, False, False)

    def top_live(state):
        n, least_top = state
        return jnp.logical_and(n < i, least_top < SB_DEAD_CARRY)

    def top_visit(state):
        n = state[0]
        return n + 1, visit(i - 1 - n, False, True)[0]

    n, least_top, _ = lax.while_loop(rest_live, full_visit, (jnp.int32(0), least_top, least_rest))
    lax.while_loop(top_live, top_visit, (n, least_top))
    for p in range(n_pairs):
        o_ref[:, p * LANES:(p + 1) * LANES] = acc_ref[p].astype(BF16)


def _sb_attention(qb, kb, vb, batch, seq):
    qb = qb.reshape(batch, seq, SB_W)
    kb = kb.reshape(batch, seq, SB_W)
    vb = vb.reshape(batch, seq, SB_W)
    assert TQ_SB == KB_SB
    q_map = lambda b, i: (b, i, 0)
    kv_map = lambda b, i: (b, 0, 0)
    return pl.pallas_call(
        _sb_kernel,
        grid=(batch, seq // TQ_SB),
        in_specs=[
            pl.BlockSpec((None, TQ_SB, SB_W), q_map),
            pl.BlockSpec((None, seq, SB_W), kv_map),
            pl.BlockSpec((None, seq, SB_W), kv_map),
        ],
        out_specs=pl.BlockSpec((None, TQ_SB, SB_W), q_map),
        out_shape=jax.ShapeDtypeStruct((batch, seq, SB_W), BF16),
        scratch_shapes=[
            pltpu.VMEM((SB_HEADS // 2, 2 * TQ_SB, LANES), BF16),
            pltpu.VMEM((SB_HEADS // 2, TQ_SB, LANES), F32),
            pltpu.VMEM((SB_HEADS // 2, 2 * TQ_SB, 1), F32),
        ],
        compiler_params=pltpu.CompilerParams(
            dimension_semantics=("arbitrary", "arbitrary"),
            vmem_limit_bytes=VMEM_LIMIT_BYTES),
        name="sb_attention",
    )(qb, kb, vb)


def _mem_kv_kernel(m_ref, g_ref, w_ref, k_ref, v_ref):
    nb, mem_len, _ = m_ref.shape
    m = m_ref[...].reshape(nb * mem_len, D_MODEL)
    u = _rms_normalize(m, g_ref[...]).astype(BF16)
    k_ref[...] = _dot(u, w_ref[:, :MEM_W]).astype(BF16).reshape(nb, mem_len, MEM_W)
    v_ref[...] = _dot(u, w_ref[:, MEM_W:]).astype(BF16).reshape(nb, mem_len, MEM_W)


def _mem_kv(mem, gain, w_bf16):
    batch, mem_len, _ = mem.shape
    nb = MEM_KV_BATCHES
    assert batch % nb == 0
    blk = lambda b: (b, 0, 0)
    return pl.pallas_call(
        _mem_kv_kernel,
        grid=(batch // nb,),
        in_specs=[
            pl.BlockSpec((nb, mem_len, D_MODEL), blk),
            pl.BlockSpec((1, D_MODEL), lambda b: (0, 0)),
            pl.BlockSpec((D_MODEL, 2 * MEM_W), lambda b: (0, 0)),
        ],
        out_specs=[pl.BlockSpec((nb, mem_len, MEM_W), blk)] * 2,
        out_shape=[jax.ShapeDtypeStruct((batch, mem_len, MEM_W), BF16)] * 2,
        compiler_params=pltpu.CompilerParams(
            dimension_semantics=("arbitrary",), vmem_limit_bytes=VMEM_LIMIT_BYTES),
        name="mem_kv",
    )(mem, gain, w_bf16)


_MEM_HEAD_LANES = [slice(h * MEM_HEAD_DIM, (h + 1) * MEM_HEAD_DIM) for h in range(MEM_HEADS)]


def _mem_scores(rows, q_ref, k_ref):
    return [_dot_nt(q_ref[rows, sl], k_ref[:, sl]) for sl in _MEM_HEAD_LANES]


def _mem_values(rows, scores, v_ref, y_ref):
    c = (MEM_HEAD_DIM ** -0.5) * math.log2(math.e)
    for sl, z in zip(_MEM_HEAD_LANES, scores):
        pr = jnp.exp2((z - jnp.max(z, axis=-1, keepdims=True)) * c)
        recip = 1.0 / jnp.sum(pr, axis=-1, keepdims=True)
        y_ref[rows, sl] = (_dot(pr.astype(BF16), v_ref[:, sl]) * recip).astype(BF16)


def _merge_kernel(sink_ref, x_ref, gpre_ref, qa_ref, kp_ref, kc_ref, vp_ref, vc_ref, bias_ref,
                  yb_ref, qm_ref, mk_ref, mv_ref, wgl_ref, wa_ref, wb_ref, wm_ref, wo_ref,
                  gpost_ref, o_ref, qs_ref, ya_ref, ym_ref, *, tiles_per_seq):
    first_of_sequence = pl.program_id(0) % tiles_per_seq == 0
    blocks_per_group = ROW_GROUP_MERGE // BLOCK
    for g, rows in enumerate(_row_groups(TM_MERGE, ROW_GROUP_MERGE)):
        blocks = range(g * blocks_per_group, (g + 1) * blocks_per_group)
        swa = [_swa_scores(j, qa_ref, kp_ref, kc_ref, vp_ref, vc_ref, qs_ref) for j in blocks]
        mem = _mem_scores(rows, qm_ref, mk_ref)
        x = x_ref[rows, :]
        u = _gained_bf16(x, gpre_ref[...])
        inv = _inv_rms(x)
        gates = [jax.nn.sigmoid(_dot(u, wgl_ref[:, n * D_MODEL:(n + 1) * D_MODEL]) * inv)
                 for n in range(N_BRANCH)]
        for j, (scores, vband) in zip(blocks, swa):
            _swa_values(j, scores, vband, sink_ref, bias_ref, first_of_sequence, ya_ref)
        _mem_values(rows, mem, mv_ref, ym_ref)
        merged = None
        for gate, y_ref, w_ref in zip(gates, (ya_ref, yb_ref, ym_ref), (wa_ref, wb_ref, wm_ref)):
            term = gate * _dot(y_ref[rows, :], w_ref[...])
            merged = term if merged is None else merged + term
        mix = _dot(merged.astype(BF16), wo_ref[...])
        o_ref[rows, :] = x + _rms_normalize(mix, gpost_ref[...])


def _merge(x2d, g_pre, sinks, bias, qa, ka, va, y_sb, qm, mk, mv, w_gl, w_swa, w_sb, w_mem, w_out,
           g_post, seq):
    t = x2d.shape[0]
    mem_len = mk.shape[1]
    assert seq % TM_MERGE == 0 and TM_MERGE % ROW_GROUP_MERGE == 0 and ROW_GROUP_MERGE % BLOCK == 0
    tiles_per_seq = seq // TM_MERGE
    blocks_per_tile = TM_MERGE // BLOCK
    row = lambda i: (i, 0)
    const = lambda i: (0, 0)
    prev = lambda i: (jnp.maximum(i * blocks_per_tile - 1, 0), 0)
    memkv = lambda i: (i // tiles_per_seq, 0, 0)
    resident = functools.partial(pl.BlockSpec, index_map=const, pipeline_mode=pl.Buffered(1))
    return pl.pallas_call(
        functools.partial(_merge_kernel, tiles_per_seq=tiles_per_seq),
        grid=(t // TM_MERGE,),
        in_specs=[
            pl.BlockSpec(memory_space=pltpu.SMEM),
            pl.BlockSpec((TM_MERGE, D_MODEL), row),
            pl.BlockSpec((1, D_MODEL), const),
            pl.BlockSpec((TM_MERGE, SWA_Q_W), row),
            pl.BlockSpec((BLOCK, SWA_KV_W), prev),
            pl.BlockSpec((TM_MERGE, SWA_KV_W), row),
            pl.BlockSpec((BLOCK, SWA_KV_W), prev),
            pl.BlockSpec((TM_MERGE, SWA_KV_W), row),
            pl.BlockSpec((SWA_Q_HEADS, BLOCK, BLOCK), lambda i: (0, 0, 0)),
            pl.BlockSpec((TM_MERGE, SB_W), row),
            pl.BlockSpec((TM_MERGE, MEM_W), row),
            pl.BlockSpec((None, mem_len, MEM_W), memkv),
            pl.BlockSpec((None, mem_len, MEM_W), memkv),
            resident((D_MODEL, N_BRANCH * D_MODEL)),
            resident((SWA_Q_W, D_MODEL)),
            resident((SB_W, D_MODEL)),
            resident((MEM_W, D_MODEL)),
            resident((D_MODEL, D_MODEL)),
            pl.BlockSpec((1, D_MODEL), const),
        ],
        out_specs=pl.BlockSpec((TM_MERGE, D_MODEL), row),
        out_shape=jax.ShapeDtypeStruct((t, D_MODEL), F32),
        scratch_shapes=[
            pltpu.VMEM((blocks_per_tile * SWA_Q_HEADS * BLOCK, LANES), BF16),
            pltpu.VMEM((TM_MERGE, SWA_Q_W), BF16),
            pltpu.VMEM((TM_MERGE, MEM_W), BF16),
        ],
        compiler_params=pltpu.CompilerParams(
            dimension_semantics=("arbitrary",), vmem_limit_bytes=VMEM_LIMIT_BYTES),
        name="merge",
    )(sinks, x2d, g_pre, qa, ka, ka, va, va, bias, y_sb, qm, mk, mv, w_gl, w_swa, w_sb, w_mem,
      w_out, g_post)


def _ffn_kernel(h_ref, gpre_ref, wg_ref, wu_ref, wd_ref, gpost_ref, o_ref):
    for rows in _row_groups(TM_FFN, ROW_GROUP_FFN):
        h = h_ref[rows, :]
        u = _gained_bf16(h, gpre_ref[...])
        inv = _inv_rms(h)
        acc = None
        for start, width in FF_CHUNKS:
            gate = _dot(u, wg_ref[:, start:start + width]) * inv
            up = _dot(u, wu_ref[:, start:start + width]) * inv
            act = (gate * jax.nn.sigmoid(gate) * up).astype(BF16)
            part = _dot(act, wd_ref[start:start + width, :])
            acc = part if acc is None else acc + part
        o_ref[rows, :] = h + _rms_normalize(acc, gpost_ref[...])


def _ffn(h2d, g_pre, w_gate, w_up, w_down, g_post):
    t = h2d.shape[0]
    row = lambda i: (i, 0)
    const = lambda i: (0, 0)
    resident = functools.partial(pl.BlockSpec, index_map=const, pipeline_mode=pl.Buffered(1))
    return pl.pallas_call(
        _ffn_kernel,
        grid=(t // TM_FFN,),
        in_specs=[
            pl.BlockSpec((TM_FFN, D_MODEL), row),
            pl.BlockSpec((1, D_MODEL), const),
            resident((D_MODEL, D_FF)),
            resident((D_MODEL, D_FF)),
            resident((D_FF, D_MODEL)),
            pl.BlockSpec((1, D_MODEL), const),
        ],
        out_specs=pl.BlockSpec((TM_FFN, D_MODEL), row),
        out_shape=jax.ShapeDtypeStruct((t, D_MODEL), F32),
        compiler_params=pltpu.CompilerParams(
            dimension_semantics=("arbitrary",), vmem_limit_bytes=VMEM_LIMIT_BYTES),
        name="ffn",
    )(h2d, g_pre, w_gate, w_up, w_down, g_post)


def _prepare_w_in(w_in):
    scale = HEAD_DIM ** -0.5
    assert scale == 0.125
    offs = np.cumsum([0] + [w for _, w in PROJ_PIECES])
    qa = w_in[:, offs[0]:offs[1]].reshape(D_MODEL, SWA_Q_HEADS, HEAD_DIM)
    qa = (qa[:, np.array(SWA_HEAD_ORDER), :] * scale).reshape(D_MODEL, SWA_Q_W)
    qb = w_in[:, offs[3]:offs[4]] * scale
    w_qkv = jnp.concatenate([qa, w_in[:, offs[1]:offs[3]], qb, w_in[:, offs[4]:QKV_W]], axis=1)
    return w_qkv.astype(BF16)


def _prepare_w_branch_swa(w):
    w = w.reshape(SWA_Q_HEADS, HEAD_DIM, D_MODEL)[np.array(SWA_HEAD_ORDER)]
    return w.reshape(SWA_Q_W, D_MODEL)


def kernel(x, mem, ln_mix_pre, ln_mix_post, w_in, swa_sinks, rel_bias, ln_mem, w_mem_kv,
           w_branch_swa, w_branch_sb, w_branch_mem, w_out, ln_ffn_pre, ln_ffn_post,
           w_gate, w_up, w_down):
    batch, seq, d = x.shape
    depth = w_in.shape[0]
    h = x.reshape(batch * seq, d)
    bias = _swa_bias_table(rel_bias)
    for l in range(depth):
        later = ((w_in[l], QKV_W), (_prepare_w_branch_swa(w_branch_swa[l]), 0),
                 (w_branch_sb[l], 0), (w_branch_mem[l], 0), (w_out[l], 0), (w_mem_kv[l], 0),
                 (w_gate[l], 0), (w_up[l], 0), (w_down[l], 0))
        (qa, ka, va, qb, kb, vb, qm), later_bf16 = _in_proj(
            h, ln_mix_pre[l][None], _prepare_w_in(w_in[l]), later)
        w_gl, w_swa, w_sb, w_mem, w_o, w_mkv, w_g, w_u, w_d = later_bf16
        y_sb = _sb_attention(qb, kb, vb, batch, seq)
        mk, mv = _mem_kv(mem, ln_mem[l][None], w_mkv)
        h = _merge(h, ln_mix_pre[l][None], swa_sinks[l].astype(F32), bias, qa, ka, va,
                   y_sb.reshape(batch * seq, SB_W), qm, mk, mv, w_gl, w_swa, w_sb, w_mem, w_o,
                   ln_mix_post[l][None], seq)
        h = _ffn(h, ln_ffn_pre[l][None], w_g, w_u, w_d, ln_ffn_post[l][None])
    return h.reshape(batch, seq, d)
```

```python
import functools
import math

import numpy as np
import jax
import jax.numpy as jnp
from jax import lax
from jax.experimental import pallas as pl
from jax.experimental.pallas import tpu as pltpu

F32 = jnp.float32
BF16 = jnp.bfloat16

D_MODEL = 1024
BLOCK = 128
EPS = 1e-6
HEAD_DIM = 64
SWA_Q_HEADS = 8
SWA_KV_HEADS = 2
SWA_WINDOW = 128
N_BUCKETS = 32
MAX_DISTANCE = 128
SB_HEADS = 8
MEM_HEADS = 4
MEM_HEAD_DIM = 128
SWA_Q_W = SWA_Q_HEADS * HEAD_DIM
SWA_KV_W = SWA_KV_HEADS * HEAD_DIM
SB_W = SB_HEADS * HEAD_DIM
MEM_W = MEM_HEADS * MEM_HEAD_DIM
N_BRANCH = 3
D_FF = 2816
LOG2_E = math.log2(math.e)

LANES = 128
BF16_SUBLANES = 16
VMEM_LIMIT_BYTES = 56 * 1024 * 1024

TM_PROJ = 1024
TM_MERGE = 1024
TM_FFN = 1024
TQ_SB = 256
KB_SB = 256
COL_CHUNK = 512
MEM_KV_BATCHES = 4
ROW_GROUP_PROJ = 512
ROW_GROUP_MERGE = 512
ROW_GROUP_FFN = 256
SB_DEAD_CARRY = 96.0
SB_TOP_ROWS = 144
FF_CHUNKS = ((0, 1024), (1024, 1024), (2048, 768))

PROJ_PIECES = (("qa", SWA_Q_W), ("ka", SWA_KV_W), ("va", SWA_KV_W), ("qb", SB_W),
               ("kb", SB_W), ("vb", SB_W), ("qm", MEM_W))
QKV_W = sum(w for _, w in PROJ_PIECES)

SWA_HEAD_ORDER = (0, 4, 1, 5, 2, 6, 3, 7)


def _inv_rms(x):
    return lax.rsqrt(jnp.mean(x * x, axis=-1, keepdims=True) + EPS)


def _rms_normalize(x, gain):
    return x * _inv_rms(x) * gain


def _gained_bf16(x, gain):
    return (x * gain).astype(BF16)


def _half_mask(shape, half):
    lane = lax.broadcasted_iota(jnp.int32, shape, len(shape) - 1)
    return (lane < HEAD_DIM) if half == 0 else (lane >= HEAD_DIM)


def _row_groups(rows, group):
    return [slice(s, s + group) for s in range(0, rows, group)]


def _dot_nt(a, b):
    return lax.dot_general(a, b, (((1,), (1,)), ((), ())), preferred_element_type=F32)


def _dot(a, b):
    return jnp.dot(a, b, preferred_element_type=F32)


def _cast_rows_per_step(rows, steps):
    for r in range(BF16_SUBLANES, rows + 1, BF16_SUBLANES):
        if rows % r == 0 and rows // r <= steps:
            return r
    raise ValueError((rows, steps))


def _in_proj_kernel(x_ref, g_ref, w_ref, *refs, casts):
    n_cast = len(casts)
    cast_in, out_refs, cast_out = refs[:n_cast], refs[n_cast:-n_cast], refs[-n_cast:]
    starts = np.cumsum([0] + [w for _, w in PROJ_PIECES])
    for rows in _row_groups(TM_PROJ, ROW_GROUP_PROJ):
        x = x_ref[rows, :]
        u = _gained_bf16(x, g_ref[...])
        inv = _inv_rms(x)
        for c0 in range(0, QKV_W, COL_CHUNK):
            c1 = min(c0 + COL_CHUNK, QKV_W)
            y = (_dot(u, w_ref[:, c0:c1]) * inv).astype(BF16)
            for out_ref, p0, p1 in zip(out_refs, starts[:-1], starts[1:]):
                lo, hi = max(c0, p0), min(c1, p1)
                if lo < hi:
                    out_ref[rows, lo - p0:hi - p0] = y[:, lo - c0:hi - c0]
    for src_ref, dst_ref, col0 in zip(cast_in, cast_out, casts):
        dst_ref[...] = src_ref[:, col0:].astype(BF16)


def _in_proj(x2d, gain, w_bf16, f32_weights):
    t = x2d.shape[0]
    steps = t // TM_PROJ
    in_w = w_bf16.shape[1]
    row = lambda i: (i, 0)
    out_shape = [jax.ShapeDtypeStruct((t, w), BF16) for _, w in PROJ_PIECES]
    out_specs = [pl.BlockSpec((TM_PROJ, w), row) for _, w in PROJ_PIECES]
    cast_in_specs, cast_out_specs, casts = [], [], []
    for w, col0 in f32_weights:
        r = _cast_rows_per_step(w.shape[0], steps)
        n = w.shape[0] // r
        slab = lambda i, n=n: (jnp.minimum(i, n - 1), 0)
        cast_in_specs.append(pl.BlockSpec((r, w.shape[1]), slab))
        cast_out_specs.append(pl.BlockSpec((r, w.shape[1] - col0), slab))
        casts.append(col0)
        out_shape.append(jax.ShapeDtypeStruct((w.shape[0], w.shape[1] - col0), BF16))
    outs = pl.pallas_call(
        functools.partial(_in_proj_kernel, casts=tuple(casts)),
        grid=(steps,),
        in_specs=[
            pl.BlockSpec((TM_PROJ, D_MODEL), row),
            pl.BlockSpec((1, D_MODEL), lambda i: (0, 0)),
            pl.BlockSpec((D_MODEL, in_w), lambda i: (0, 0), pipeline_mode=pl.Buffered(1)),
        ] + cast_in_specs,
        out_specs=out_specs + cast_out_specs,
        out_shape=out_shape,
        compiler_params=pltpu.CompilerParams(
            dimension_semantics=("arbitrary",), vmem_limit_bytes=VMEM_LIMIT_BYTES),
        name="in_proj",
    )(x2d, gain, w_bf16, *[w for w, _ in f32_weights])
    return outs[:len(PROJ_PIECES)], outs[len(PROJ_PIECES):]


def _swa_scores(j, q_ref, kp_ref, kc_ref, vp_ref, vc_ref, qs_ref):
    half0 = _half_mask((BLOCK, LANES), 0)
    rows = SWA_Q_HEADS * BLOCK
    blk = slice(j * BLOCK, (j + 1) * BLOCK)
    for p in range(SWA_Q_HEADS // 2):
        q2 = q_ref[blk, p * LANES:(p + 1) * LANES]
        zero = jnp.zeros_like(q2)
        base = j * rows + 2 * p * BLOCK
        qs_ref[base:base + BLOCK, :] = jnp.where(half0, q2, zero)
        qs_ref[base + BLOCK:base + 2 * BLOCK, :] = jnp.where(half0, zero, q2)
    if j == 0:
        kband = jnp.concatenate([kp_ref[...], kc_ref[blk, :]], axis=0)
        vband = jnp.concatenate([vp_ref[...], vc_ref[blk, :]], axis=0)
    else:
        band = slice((j - 1) * BLOCK, (j + 1) * BLOCK)
        kband, vband = kc_ref[band, :], vc_ref[band, :]
    return _dot_nt(kband, qs_ref[j * rows:(j + 1) * rows, :]), vband


def _swa_values(j, scores, vband, sink_ref, bias_ref, first_of_sequence, y_ref):
    key = lax.broadcasted_iota(jnp.int32, (BLOCK, BLOCK), 0)
    qry = lax.broadcasted_iota(jnp.int32, (BLOCK, BLOCK), 1)
    from_prev = key > qry
    first_kv = key < HEAD_DIM
    weights, recips = [], []
    for h in range(SWA_Q_HEADS):
        head = SWA_HEAD_ORDER[h]
        zh = scores[:, h * BLOCK:(h + 1) * BLOCK]
        s = jnp.where(from_prev, zh[:BLOCK], zh[BLOCK:]) + bias_ref[head]
        if j == 0:
            s = jnp.where(jnp.logical_and(from_prev, first_of_sequence), -jnp.inf, s)
        sink = sink_ref[head]
        m = jnp.maximum(jnp.max(s, axis=0, keepdims=True), sink)
        pr = jnp.exp(s - m)
        denom = jnp.sum(pr, axis=0, keepdims=True) + jnp.exp(sink - m)
        recips.append(1.0 / denom)
        zero = jnp.zeros_like(pr)
        unfolded = jnp.concatenate(
            [jnp.where(from_prev, pr, zero), jnp.where(from_prev, zero, pr)], axis=0)
        weights.append(unfolded.astype(BF16))
    v_t = vband.astype(F32).T.astype(BF16)
    o_t = _dot(v_t, jnp.concatenate(weights, axis=1))
    for p in range(SWA_Q_HEADS // 2):
        t0 = o_t[:, (2 * p) * BLOCK:(2 * p + 1) * BLOCK] * recips[2 * p]
        t1 = o_t[:, (2 * p + 1) * BLOCK:(2 * p + 2) * BLOCK] * recips[2 * p + 1]
        tile = jnp.where(first_kv, t0, t1).T
        y_ref[j * BLOCK:(j + 1) * BLOCK, p * LANES:(p + 1) * LANES] = tile.astype(BF16)


def _t5_bucket(dist):
    max_exact = N_BUCKETS // 2
    d = np.maximum(dist, 0)
    df = np.maximum(d, 1).astype(np.float32)
    large = max_exact + (np.log(df / np.float32(max_exact))
                         / np.float32(math.log(MAX_DISTANCE / max_exact))
                         * np.float32(N_BUCKETS - max_exact)).astype(np.int32)
    large = np.minimum(large, N_BUCKETS - 1)
    return np.where(d < max_exact, d, large)


def _swa_bias_table(rel_bias):
    c = np.arange(BLOCK)[:, None]
    r = np.arange(BLOCK)[None, :]
    dist = np.where(c > r, r + BLOCK - c, r - c)
    assert SWA_WINDOW == BLOCK
    bucket = _t5_bucket(dist)
    rb = rel_bias.astype(F32)
    bias = jnp.zeros((SWA_Q_HEADS, BLOCK, BLOCK), F32)
    for b in range(N_BUCKETS):
        bias = jnp.where((bucket == b)[None], rb[b][:, None, None], bias)
    return bias


def _sb_kernel(q_ref, k_ref, v_ref, o_ref, qs_ref, acc_ref, carry_ref):
    i = pl.program_id(1)
    n_pairs = SB_HEADS // 2
    top, rest = SB_TOP_ROWS, TQ_SB - SB_TOP_ROWS
    rows = 2 * TQ_SB
    half0_top = _half_mask((top, LANES), 0)
    half0_rest = _half_mask((rest, LANES), 0)
    r = lax.broadcasted_iota(jnp.int32, (KB_SB, KB_SB), 0)
    c = lax.broadcasted_iota(jnp.int32, (KB_SB, KB_SB), 1)
    suffix = jnp.where(r > c, 1.0, 0.0).astype(BF16)
    rr = lax.broadcasted_iota(jnp.int32, (rows, KB_SB), 0)
    cc = lax.broadcasted_iota(jnp.int32, (rows, KB_SB), 1)
    tile_row = jnp.where(rr < top, rr, jnp.where(rr < 2 * top + rest, rr - top, rr - top - rest))
    causal = cc < tile_row

    for p in range(n_pairs):
        q2 = q_ref[:, p * LANES:(p + 1) * LANES]
        q_top, q_rest = q2[:top], q2[top:]
        qs_ref[p, 0:top, :] = jnp.where(half0_top, q_top, jnp.zeros_like(q_top))
        qs_ref[p, top:2 * top, :] = jnp.where(half0_top, jnp.zeros_like(q_top), q_top)
        qs_ref[p, 2 * top:2 * top + rest, :] = jnp.where(half0_rest, q_rest, jnp.zeros_like(q_rest))
        qs_ref[p, 2 * top + rest:, :] = jnp.where(half0_rest, jnp.zeros_like(q_rest), q_rest)
    lanes = [slice(p * LANES, (p + 1) * LANES) for p in range(n_pairs)]

    def visit(j, diagonal, top_only):
        n = 2 * top if top_only else rows
        start = pl.multiple_of(j * KB_SB, KB_SB)
        zs = [_dot_nt(qs_ref[p, 0:n, :], k_ref[pl.ds(start, KB_SB), lanes[p]])
              for p in range(n_pairs)]
        sps, lzs, firsts = [], [], []
        for z in zs:
            sp = jnp.maximum(z, 0.0) + jnp.log(1.0 + jnp.exp2(jnp.abs(z) * -LOG2_E))
            lz = z - sp
            if diagonal:
                sp = jnp.where(causal, sp, 0.0)
                lz = jnp.where(causal, lz, -jnp.inf)
            sps.append(sp.astype(BF16))
            lzs.append(lz)
            firsts.append(sp[:, 0:1])
        within_all = _dot(jnp.concatenate(sps, axis=0), suffix)
        least = None
        for p in range(n_pairs):
            within = within_all[p * n:(p + 1) * n]
            total = within[:, 0:1] + firsts[p]
            if diagonal:
                a = jnp.exp(lzs[p] - within)
            else:
                carry = carry_ref[p, 0:n, :]
                a = jnp.exp(lzs[p] - (within + carry))
                total = carry + total
            least = total if least is None else jnp.minimum(least, total)
            pv = _dot(a.astype(BF16), v_ref[pl.ds(start, KB_SB), lanes[p]])
            upd_top = jnp.where(half0_top, pv[0:top], pv[top:2 * top])
            if top_only:
                acc_ref[p, 0:top, :] += upd_top
            else:
                upd_rest = jnp.where(half0_rest, pv[2 * top:2 * top + rest], pv[2 * top + rest:])
                upd = jnp.concatenate([upd_top, upd_rest], axis=0)
                if diagonal:
                    acc_ref[p] = upd
                else:
                    acc_ref[p] += upd
            carry_ref[p, 0:n, :] = total
        least_top = jnp.min(least[0:2 * top])
        least_rest = jnp.float32(jnp.inf) if top_only else jnp.min(least[2 * top:])
        return least_top, least_rest

    least_top, least_rest = visit(i, True, False)

    def rest_live(state):
        n, _, least_rest = state
        return jnp.logical_and(n < i, least_rest < SB_DEAD_CARRY)

    def full_visit(state):
        n = state[0]
        return (n + 1,) + visit(i - 1 - n, False, False)

    def top_live(state):
        n, least_top = state
        return jnp.logical_and(n < i, least_top < SB_DEAD_CARRY)

    def top_visit(state):
        n = state[0]
        return n + 1, visit(i - 1 - n, False, True)[0]

    n, least_top, _ = lax.while_loop(rest_live, full_visit, (jnp.int32(0), least_top, least_rest))
    lax.while_loop(top_live, top_visit, (n, least_top))
    for p in range(n_pairs):
        o_ref[:, p * LANES:(p + 1) * LANES] = acc_ref[p].astype(BF16)


def _sb_attention(qb, kb, vb, batch, seq):
    qb = qb.reshape(batch, seq, SB_W)
    kb = kb.reshape(batch, seq, SB_W)
    vb = vb.reshape(batch, seq, SB_W)
    assert TQ_SB == KB_SB
    q_map = lambda b, i: (b, i, 0)
    kv_map = lambda b, i: (b, 0, 0)
    return pl.pallas_call(
        _sb_kernel,
        grid=(batch, seq // TQ_SB),
        in_specs=[
            pl.BlockSpec((None, TQ_SB, SB_W), q_map),
            pl.BlockSpec((None, seq, SB_W), kv_map),
            pl.BlockSpec((None, seq, SB_W), kv_map),
        ],
        out_specs=pl.BlockSpec((None, TQ_SB, SB_W), q_map),
        out_shape=jax.ShapeDtypeStruct((batch, seq, SB_W), BF16),
        scratch_shapes=[
            pltpu.VMEM((SB_HEADS // 2, 2 * TQ_SB, LANES), BF16),
            pltpu.VMEM((SB_HEADS // 2, TQ_SB, LANES), F32),
            pltpu.VMEM((SB_HEADS // 2, 2 * TQ_SB, 1), F32),
        ],
        compiler_params=pltpu.CompilerParams(
            dimension_semantics=("arbitrary", "arbitrary"),
            vmem_limit_bytes=VMEM_LIMIT_BYTES),
        name="sb_attention",
    )(qb, kb, vb)


def _mem_kv_kernel(m_ref, g_ref, w_ref, k_ref, v_ref):
    nb, mem_len, _ = m_ref.shape
    m = m_ref[...].reshape(nb * mem_len, D_MODEL)
    u = _rms_normalize(m, g_ref[...]).astype(BF16)
    k_ref[...] = _dot(u, w_ref[:, :MEM_W]).astype(BF16).reshape(nb, mem_len, MEM_W)
    v_ref[...] = _dot(u, w_ref[:, MEM_W:]).astype(BF16).reshape(nb, mem_len, MEM_W)


def _mem_kv(mem, gain, w_bf16):
    batch, mem_len, _ = mem.shape
    nb = MEM_KV_BATCHES
    assert batch % nb == 0
    blk = lambda b: (b, 0, 0)
    return pl.pallas_call(
        _mem_kv_kernel,
        grid=(batch // nb,),
        in_specs=[
            pl.BlockSpec((nb, mem_len, D_MODEL), blk),
            pl.BlockSpec((1, D_MODEL), lambda b: (0, 0)),
            pl.BlockSpec((D_MODEL, 2 * MEM_W), lambda b: (0, 0)),
        ],
        out_specs=[pl.BlockSpec((nb, mem_len, MEM_W), blk)] * 2,
        out_shape=[jax.ShapeDtypeStruct((batch, mem_len, MEM_W), BF16)] * 2,
        compiler_params=pltpu.CompilerParams(
            dimension_semantics=("arbitrary",), vmem_limit_bytes=VMEM_LIMIT_BYTES),
        name="mem_kv",
    )(mem, gain, w_bf16)


_MEM_HEAD_LANES = [slice(h * MEM_HEAD_DIM, (h + 1) * MEM_HEAD_DIM) for h in range(MEM_HEADS)]


def _mem_scores(rows, q_ref, k_ref):
    return [_dot_nt(q_ref[rows, sl], k_ref[:, sl]) for sl in _MEM_HEAD_LANES]


def _mem_values(rows, scores, v_ref, y_ref):
    c = (MEM_HEAD_DIM ** -0.5) * math.log2(math.e)
    for sl, z in zip(_MEM_HEAD_LANES, scores):
        pr = jnp.exp2((z - jnp.max(z, axis=-1, keepdims=True)) * c)
        recip = 1.0 / jnp.sum(pr, axis=-1, keepdims=True)
        y_ref[rows, sl] = (_dot(pr.astype(BF16), v_ref[:, sl]) * recip).astype(BF16)


def _merge_kernel(sink_ref, x_ref, gpre_ref, qa_ref, kp_ref, kc_ref, vp_ref, vc_ref, bias_ref,
                  yb_ref, qm_ref, mk_ref, mv_ref, wgl_ref, wa_ref, wb_ref, wm_ref, wo_ref,
                  gpost_ref, o_ref, qs_ref, ya_ref, ym_ref, *, tiles_per_seq):
    first_of_sequence = pl.program_id(0) % tiles_per_seq == 0
    blocks_per_group = ROW_GROUP_MERGE // BLOCK
    for g, rows in enumerate(_row_groups(TM_MERGE, ROW_GROUP_MERGE)):
        blocks = range(g * blocks_per_group, (g + 1) * blocks_per_group)
        swa = [_swa_scores(j, qa_ref, kp_ref, kc_ref, vp_ref, vc_ref, qs_ref) for j in blocks]
        mem = _mem_scores(rows, qm_ref, mk_ref)
        x = x_ref[rows, :]
        u = _gained_bf16(x, gpre_ref[...])
        inv = _inv_rms(x)
        half = len(blocks) // 2
        shares = (zip(blocks[:half], swa[:half]), zip(blocks[half:], swa[half:]), ())
        gates = []
        for n in range(N_BRANCH):
            gates.append(jax.nn.sigmoid(
                _dot(u, wgl_ref[:, n * D_MODEL:(n + 1) * D_MODEL]) * inv))
            for j, (scores, vband) in shares[n]:
                _swa_values(j, scores, vband, sink_ref, bias_ref, first_of_sequence, ya_ref)
        _mem_values(rows, mem, mv_ref, ym_ref)
        merged = None
        for gate, y_ref, w_ref in zip(gates, (ya_ref, yb_ref, ym_ref), (wa_ref, wb_ref, wm_ref)):
            term = gate * _dot(y_ref[rows, :], w_ref[...])
            merged = term if merged is None else merged + term
        mix = _dot(merged.astype(BF16), wo_ref[...])
        o_ref[rows, :] = x + _rms_normalize(mix, gpost_ref[...])


def _merge(x2d, g_pre, sinks, bias, qa, ka, va, y_sb, qm, mk, mv, w_gl, w_swa, w_sb, w_mem, w_out,
           g_post, seq):
    t = x2d.shape[0]
    mem_len = mk.shape[1]
    assert seq % TM_MERGE == 0 and TM_MERGE % ROW_GROUP_MERGE == 0 and ROW_GROUP_MERGE % BLOCK == 0
    tiles_per_seq = seq // TM_MERGE
    blocks_per_tile = TM_MERGE // BLOCK
    row = lambda i: (i, 0)
    const = lambda i: (0, 0)
    prev = lambda i: (jnp.maximum(i * blocks_per_tile - 1, 0), 0)
    memkv = lambda i: (i // tiles_per_seq, 0, 0)
    resident = functools.partial(pl.BlockSpec, index_map=const, pipeline_mode=pl.Buffered(1))
    return pl.pallas_call(
        functools.partial(_merge_kernel, tiles_per_seq=tiles_per_seq),
        grid=(t // TM_MERGE,),
        in_specs=[
            pl.BlockSpec(memory_space=pltpu.SMEM),
            pl.BlockSpec((TM_MERGE, D_MODEL), row),
            pl.BlockSpec((1, D_MODEL), const),
            pl.BlockSpec((TM_MERGE, SWA_Q_W), row),
            pl.BlockSpec((BLOCK, SWA_KV_W), prev),
            pl.BlockSpec((TM_MERGE, SWA_KV_W), row),
            pl.BlockSpec((BLOCK, SWA_KV_W), prev),
            pl.BlockSpec((TM_MERGE, SWA_KV_W), row),
            pl.BlockSpec((SWA_Q_HEADS, BLOCK, BLOCK), lambda i: (0, 0, 0)),
            pl.BlockSpec((TM_MERGE, SB_W), row),
            pl.BlockSpec((TM_MERGE, MEM_W), row),
            pl.BlockSpec((None, mem_len, MEM_W), memkv),
            pl.BlockSpec((None, mem_len, MEM_W), memkv),
            resident((D_MODEL, N_BRANCH * D_MODEL)),
            resident((SWA_Q_W, D_MODEL)),
            resident((SB_W, D_MODEL)),
            resident((MEM_W, D_MODEL)),
            resident((D_MODEL, D_MODEL)),
            pl.BlockSpec((1, D_MODEL), const),
        ],
        out_specs=pl.BlockSpec((TM_MERGE, D_MODEL), row),
        out_shape=jax.ShapeDtypeStruct((t, D_MODEL), F32),
        scratch_shapes=[
            pltpu.VMEM((blocks_per_tile * SWA_Q_HEADS * BLOCK, LANES), BF16),
            pltpu.VMEM((TM_MERGE, SWA_Q_W), BF16),
            pltpu.VMEM((TM_MERGE, MEM_W), BF16),
        ],
        compiler_params=pltpu.CompilerParams(
            dimension_semantics=("arbitrary",), vmem_limit_bytes=VMEM_LIMIT_BYTES),
        name="merge",
    )(sinks, x2d, g_pre, qa, ka, ka, va, va, bias, y_sb, qm, mk, mv, w_gl, w_swa, w_sb, w_mem,
      w_out, g_post)


def _ffn_kernel(h_ref, gpre_ref, wg_ref, wu_ref, wd_ref, gpost_ref, o_ref):
    for rows in _row_groups(TM_FFN, ROW_GROUP_FFN):
        h = h_ref[rows, :]
        u = _gained_bf16(h, gpre_ref[...])
        inv = _inv_rms(h)
        acc = None
        for start, width in FF_CHUNKS:
            gate = _dot(u, wg_ref[:, start:start + width]) * inv
            up = _dot(u, wu_ref[:, start:start + width]) * inv
            act = (gate * jax.nn.sigmoid(gate) * up).astype(BF16)
            part = _dot(act, wd_ref[start:start + width, :])
            acc = part if acc is None else acc + part
        o_ref[rows, :] = h + _rms_normalize(acc, gpost_ref[...])


def _ffn(h2d, g_pre, w_gate, w_up, w_down, g_post):
    t = h2d.shape[0]
    row = lambda i: (i, 0)
    const = lambda i: (0, 0)
    resident = functools.partial(pl.BlockSpec, index_map=const, pipeline_mode=pl.Buffered(1))
    return pl.pallas_call(
        _ffn_kernel,
        grid=(t // TM_FFN,),
        in_specs=[
            pl.BlockSpec((TM_FFN, D_MODEL), row),
            pl.BlockSpec((1, D_MODEL), const),
            resident((D_MODEL, D_FF)),
            resident((D_MODEL, D_FF)),
            resident((D_FF, D_MODEL)),
            pl.BlockSpec((1, D_MODEL), const),
        ],
        out_specs=pl.BlockSpec((TM_FFN, D_MODEL), row),
        out_shape=jax.ShapeDtypeStruct((t, D_MODEL), F32),
        compiler_params=pltpu.CompilerParams(
            dimension_semantics=("arbitrary",), vmem_limit_bytes=VMEM_LIMIT_BYTES),
        name="ffn",
    )(h2d, g_pre, w_gate, w_up, w_down, g_post)


def _prepare_w_in(w_in):
    scale = HEAD_DIM ** -0.5
    assert scale == 0.125
    offs = np.cumsum([0] + [w for _, w in PROJ_PIECES])
    qa = w_in[:, offs[0]:offs[1]].reshape(D_MODEL, SWA_Q_HEADS, HEAD_DIM)
    qa = (qa[:, np.array(SWA_HEAD_ORDER), :] * scale).reshape(D_MODEL, SWA_Q_W)
    qb = w_in[:, offs[3]:offs[4]] * scale
    w_qkv = jnp.concatenate([qa, w_in[:, offs[1]:offs[3]], qb, w_in[:, offs[4]:QKV_W]], axis=1)
    return w_qkv.astype(BF16)


def _prepare_w_branch_swa(w):
    w = w.reshape(SWA_Q_HEADS, HEAD_DIM, D_MODEL)[np.array(SWA_HEAD_ORDER)]
    return w.reshape(SWA_Q_W, D_MODEL)


def kernel(x, mem, ln_mix_pre, ln_mix_post, w_in, swa_sinks, rel_bias, ln_mem, w_mem_kv,
           w_branch_swa, w_branch_sb, w_branch_mem, w_out, ln_ffn_pre, ln_ffn_post,
           w_gate, w_up, w_down):
    batch, seq, d = x.shape
    depth = w_in.shape[0]
    h = x.reshape(batch * seq, d)
    bias = _swa_bias_table(rel_bias)
    for l in range(depth):
        later = ((w_in[l], QKV_W), (_prepare_w_branch_swa(w_branch_swa[l]), 0),
                 (w_branch_sb[l], 0), (w_branch_mem[l], 0), (w_out[l], 0), (w_mem_kv[l], 0),
                 (w_gate[l], 0), (w_up[l], 0), (w_down[l], 0))
        (qa, ka, va, qb, kb, vb, qm), later_bf16 = _in_proj(
            h, ln_mix_pre[l][None], _prepare_w_in(w_in[l]), later)
        w_gl, w_swa, w_sb, w_mem, w_o, w_mkv, w_g, w_u, w_d = later_bf16
        y_sb = _sb_attention(qb, kb, vb, batch, seq)
        mk, mv = _mem_kv(mem, ln_mem[l][None], w_mkv)
        h = _merge(h, ln_mix_pre[l][None], swa_sinks[l].astype(F32), bias, qa, ka, va,
                   y_sb.reshape(batch * seq, SB_W), qm, mk, mv, w_gl, w_swa, w_sb, w_mem, w_o,
                   ln_mix_post[l][None], seq)
        h = _ffn(h, ln_ffn_pre[l][None], w_g, w_u, w_d, ln_ffn_post[l][None])
    return h.reshape(batch, seq, d)
```
